```python
import jax
import jax.numpy as jnp
from jax import lax
import numpy as np

D_MODEL = 1024
BATCH = 8
SEQ = 2048
DEPTH = 4
DEC_BATCH = 128
DEC_SEQ = 1
PAST_LEN = 2048
PAGE_SIZE = 128

N_MIXERS = 3
N_HEADS = 16
HEAD_DIM = D_MODEL // N_HEADS
ATTN_WIDTH = N_HEADS * HEAD_DIM
DIL_PAIRS = ((128, 1), (512, 4), (2048, 16))
N_DIL = len(DIL_PAIRS)
BAND_BLOCK = 128
ALIBI_MAX_EXP = 8.0
POOL_WINDOWS = (2, 4, 8, 16)
POOL_GROUPS = len(POOL_WINDOWS)
POOL_CH = D_MODEL // POOL_GROUPS
POOL_STATE = max(POOL_WINDOWS) - 1
CONV_WIDTH = 3
N_EXPERT_GROUPS = 4
EXPERTS_PER_GROUP = 8
N_EXPERTS = N_EXPERT_GROUPS * EXPERTS_PER_GROUP
EXPERT_FF = D_MODEL // 4
TOP_K_INNER = 2
DEEPNORM_ALPHA = (2 * DEPTH) ** 0.25
DEEPNORM_BETA = (8 * DEPTH) ** -0.25
LN_EPS = 1e-5
NEG_INF = -1e30
N_A_LAYERS = len(range(0, DEPTH, N_MIXERS))
N_B_LAYERS = len(range(1, DEPTH, N_MIXERS))
N_C_LAYERS = len(range(2, DEPTH, N_MIXERS))

kernel_name = 'hybrid_dilated_pool_conv_hmoe_step'


def alibi_slopes():
    h = jnp.arange(1, N_HEADS + 1, dtype=jnp.float32)
    return jnp.exp2(-ALIBI_MAX_EXP * h / N_HEADS)


def softmax_lse(s):
    m = jnp.max(s, axis=-1, keepdims=True)
    e = jnp.exp(s - m)
    den = jnp.sum(e, axis=-1, keepdims=True)
    return e / den, (m + jnp.log(den))[..., 0]


def layer_norm(x, g, b):
    xf = x.astype(jnp.float32)
    mu = jnp.mean(xf, axis=-1, keepdims=True)
    xc = xf - mu
    var = jnp.mean(xc * xc, axis=-1, keepdims=True)
    return (xc * lax.rsqrt(var + LN_EPS) * g + b).astype(x.dtype)


def dilated_band_attention(q, k, v, dil, n_steps, slopes):
    n, s, h, dh = q.shape
    span = dil * BAND_BLOCK
    s_pad = -(-s // span) * span
    n_sub = s_pad // dil
    nb = n_sub // BAND_BLOCK

    def strided(a):
        a = jnp.pad(a, ((0, 0), (0, s_pad - s), (0, 0), (0, 0)))
        a = a.reshape(n, n_sub, dil, h, dh).transpose(0, 2, 1, 3, 4)
        return a.reshape(n, dil, nb, BAND_BLOCK, h, dh)

    def with_prev(a):
        prev = jnp.pad(a, ((0, 0), (0, 0), (1, 0), (0, 0), (0, 0), (0, 0)))[:, :, :-1]
        return jnp.concatenate([prev, a], axis=3)

    qs = strided(q)
    kb = with_prev(strided(k))
    vb = with_prev(strided(v))
    scores = jnp.einsum('nrbqhd,nrbkhd->nrbhqk', qs, kb).astype(jnp.float32) * (HEAD_DIM ** -0.5)
    qi = jnp.arange(BAND_BLOCK)[:, None]
    kj = jnp.arange(2 * BAND_BLOCK)[None, :]
    dist = qi + BAND_BLOCK - kj
    key_sub = jnp.arange(nb)[:, None, None] * BAND_BLOCK - BAND_BLOCK + kj[None]
    valid = (dist >= 0) & (dist <= n_steps) & (key_sub >= 0)
    bias = -slopes[:, None, None] * (dist * dil).astype(jnp.float32)[None]
    scores = jnp.where(valid[:, None], scores + bias, NEG_INF)
    p, lse = softmax_lse(scores)
    o = jnp.einsum('nrbhqk,nrbkhd->nrbqhd', p, vb.astype(jnp.float32))
    o = o.reshape(n, dil, n_sub, h, dh).transpose(0, 2, 1, 3, 4).reshape(n, s_pad, h, dh)[:, :s]
    lse = lse.transpose(0, 1, 2, 4, 3).reshape(n, dil, n_sub, h)
    lse = lse.transpose(0, 2, 1, 3).reshape(n, s_pad, h)[:, :s]
    return o, lse


def dilated_gather_attention(q, kv_new, kv_buf, dil, n_steps, slopes):
    buf_len = kv_buf.shape[1]
    n_new = q.shape[1]
    steps = jnp.arange(n_steps + 1)
    idx = (buf_len + jnp.arange(n_new))[:, None] - steps[None, :] * dil
    g_buf = kv_buf[:, jnp.clip(idx, 0, buf_len - 1)]
    g_new = kv_new[:, jnp.clip(idx - buf_len, 0, n_new - 1)].astype(kv_buf.dtype)
    kv = jnp.where((idx < buf_len)[None, :, :, None, None, None], g_buf, g_new)
    scores = jnp.einsum('bqhd,bqkhd->bqhk', q, kv[:, :, :, 0]).astype(jnp.float32) * (HEAD_DIM ** -0.5)
    bias = -slopes[:, None] * (steps * dil).astype(jnp.float32)[None, :]
    scores = jnp.where((idx >= 0)[None, :, None, :], scores + bias, NEG_INF)
    p, lse = softmax_lse(scores)
    o = jnp.einsum('bqhk,bqkhd->bqhd', p, kv[:, :, :, 1].astype(jnp.float32))
    return o, lse


def mixer_attention(x, kv_bufs, w_in, w_out, slopes):
    n_seq, n_pos, _ = x.shape
    proj = (x @ w_in).reshape(n_seq, n_pos, N_DIL, 3, N_HEADS, HEAD_DIM)
    outs, lses, rows = [], [], []
    for g, (win, dil) in enumerate(DIL_PAIRS):
        q, k, v = proj[:, :, g, 0], proj[:, :, g, 1], proj[:, :, g, 2]
        kv = jnp.stack([k, v], axis=2)
        if kv_bufs is None:
            o, l = dilated_band_attention(q, k, v, dil, win // dil, slopes)
            rows.append(kv[:, n_pos - min(win, n_pos):])
        else:
            o, l = dilated_gather_attention(q, kv, kv_bufs[g], dil, win // dil, slopes)
            rows.append(kv)
        outs.append(o)
        lses.append(l)
    w = jax.nn.softmax(jnp.stack(lses), axis=0)
    o = jnp.einsum('gnsh,gnshd->nshd', w, jnp.stack(outs))
    y = o.reshape(n_seq, n_pos, ATTN_WIDTH).astype(x.dtype) @ w_out
    return y, rows


def mixer_pool(x, prev, pos0, w_in, w_group, scale, w_out):
    n_seq, n_pos, _ = x.shape
    u = x @ w_in
    ext = jnp.concatenate([prev.astype(u.dtype), u], axis=1)
    cs = jnp.pad(jnp.cumsum(ext.astype(jnp.float32), axis=1), ((0, 0), (1, 0), (0, 0)))
    end = cs[:, POOL_STATE + 1:]
    pos = pos0 + jnp.arange(n_pos)
    pooled = []
    for g, win in enumerate(POOL_WINDOWS):
        ch = slice(g * POOL_CH, (g + 1) * POOL_CH)
        start = cs[:, POOL_STATE + 1 - win:POOL_STATE + 1 - win + n_pos, ch]
        cnt = jnp.minimum(win, pos + 1).astype(jnp.float32)[None, :, None]
        pooled.append((end[..., ch] - start) / cnt)
    pooled = jnp.concatenate(pooled, axis=-1) - u.astype(jnp.float32)
    pooled = pooled.astype(x.dtype).reshape(n_seq, n_pos, POOL_GROUPS, POOL_CH)
    z = jnp.einsum('nsgc,gcd->nsgd', pooled, w_group).reshape(n_seq, n_pos, D_MODEL)
    y = (z * scale) @ w_out
    return y, ext[:, -POOL_STATE:]


def mixer_conv(x, prev, w_in, conv_w, w_out):
    n_pos = x.shape[1]
    b, c, h = jnp.split(x @ w_in, 3, axis=-1)
    ext = jnp.concatenate([prev.astype(x.dtype), c * h], axis=1)
    conv = sum(conv_w[j] * ext[:, j:j + n_pos] for j in range(CONV_WIDTH))
    y = (b * conv) @ w_out
    return y, ext[:, -(CONV_WIDTH - 1):]


def hier_moe(x, w_rg, b_rg, w_re, b_re, w_up, w_down):
    shape = x.shape
    t = x.reshape(-1, D_MODEL)
    g_logits = (t @ w_rg).astype(jnp.float32) + b_rg
    g_prob = jax.nn.softmax(g_logits, axis=-1)
    g_sel = jnp.argmax(g_logits, axis=-1)
    p_g = jnp.take_along_axis(g_prob, g_sel[:, None], axis=1)
    e_logits = jnp.einsum('td,gde->tge', t, w_re).astype(jnp.float32) + b_re
    e_logits = jnp.take_along_axis(e_logits, g_sel[:, None, None], axis=1)[:, 0]
    top_v, top_i = lax.top_k(e_logits, TOP_K_INNER)
    q = jax.nn.softmax(top_v, axis=-1)
    inner = jnp.sum(jax.nn.one_hot(top_i, EXPERTS_PER_GROUP, dtype=jnp.float32) * q[..., None], axis=1)
    gates = jax.nn.one_hot(g_sel, N_EXPERT_GROUPS, dtype=jnp.float32)[:, :, None] * (p_g * inner)[:, None, :]
    gates = gates.reshape(-1, N_EXPERTS).astype(x.dtype)
    a, u = jnp.split(jnp.einsum('td,edf->tef', t, w_up), 2, axis=-1)
    hid = jax.nn.silu(a) * u * gates[..., None]
    y = jnp.einsum('tef,efd->td', hid, w_down)
    return y.reshape(shape)


def trunk(x, pos0, kv_bufs, pool_prev, conv_prev, p):
    slopes = alibi_slopes()
    kv_rows = tuple([] for _ in DIL_PAIRS)
    pool_rows, conv_rows = [], []
    for i in range(DEPTH):
        j, kind = i // N_MIXERS, i % N_MIXERS
        if kind == 0:
            bufs = None if kv_bufs is None else [c[j] for c in kv_bufs]
            y, rows = mixer_attention(x, bufs, p['a_w_in'][j], p['a_w_out'][j], slopes)
            for g in range(N_DIL):
                kv_rows[g].append(rows[g])
        elif kind == 1:
            y, st = mixer_pool(x, pool_prev[j], pos0, p['b_w_in'][j], p['b_w_group'][j],
                               p['b_scale'][j], p['b_w_out'][j])
            pool_rows.append(st)
        else:
            y, st = mixer_conv(x, conv_prev[j], p['c_w_in'][j], p['c_conv'][j], p['c_w_out'][j])
            conv_rows.append(st)
        x = layer_norm(DEEPNORM_ALPHA * x + y, p['ln1_g'][i], p['ln1_b'][i])
        f = hier_moe(x, p['moe_w_group'][i], p['moe_b_group'][i], p['moe_w_expert'][i],
                     p['moe_b_expert'][i], p['moe_w_up'][i], p['moe_w_down'][i])
        x = layer_norm(DEEPNORM_ALPHA * x + f, p['ln2_g'][i], p['ln2_b'][i])
    return x, [jnp.stack(r) for r in kv_rows], jnp.stack(pool_rows), jnp.stack(conv_rows)


def setup_inputs(seed: int = 0) -> dict:
    key = jax.random.key(seed)
    keys = iter(jax.random.split(key, 32))

    def nrm(shape, scale):
        return jax.random.normal(next(keys), shape, jnp.float32) * scale

    lens = [min(w, PAST_LEN) for w, _ in DIL_PAIRS]
    d = D_MODEL
    return {
        'x_prompt': nrm((BATCH, SEQ, d), 1.0),
        'x_sample': nrm((DEC_BATCH, DEC_SEQ, d), 1.0),
        'cache_kv_w128': nrm((N_A_LAYERS, DEC_BATCH, lens[0], 2, N_HEADS, HEAD_DIM), 1.0),
        'cache_kv_w512': nrm((N_A_LAYERS, DEC_BATCH, lens[1], 2, N_HEADS, HEAD_DIM), 1.0),
        'cache_kv_w2048': nrm((N_A_LAYERS, DEC_BATCH, lens[2], 2, N_HEADS, HEAD_DIM), 1.0),
        'state_pool': nrm((N_B_LAYERS, DEC_BATCH, POOL_STATE, d), 1.0),
        'state_conv': nrm((N_C_LAYERS, DEC_BATCH, CONV_WIDTH - 1, d), 1.0),
        'a_w_in': nrm((N_A_LAYERS, d, N_DIL * 3 * ATTN_WIDTH), d ** -0.5),
        'a_w_out': nrm((N_A_LAYERS, ATTN_WIDTH, d), ATTN_WIDTH ** -0.5 * DEEPNORM_BETA),
        'b_w_in': nrm((N_B_LAYERS, d, d), d ** -0.5),
        'b_w_group': nrm((N_B_LAYERS, POOL_GROUPS, POOL_CH, POOL_CH), POOL_CH ** -0.5),
        'b_scale': 1.0 + nrm((N_B_LAYERS, d), 0.02),
        'b_w_out': nrm((N_B_LAYERS, d, d), d ** -0.5 * DEEPNORM_BETA),
        'c_w_in': nrm((N_C_LAYERS, d, 3 * d), d ** -0.5),
        'c_conv': nrm((N_C_LAYERS, CONV_WIDTH, d), CONV_WIDTH ** -0.5),
        'c_w_out': nrm((N_C_LAYERS, d, d), d ** -0.5 * DEEPNORM_BETA),
        'ln1_g': 1.0 + nrm((DEPTH, d), 0.02),
        'ln1_b': nrm((DEPTH, d), 0.02),
        'ln2_g': 1.0 + nrm((DEPTH, d), 0.02),
        'ln2_b': nrm((DEPTH, d), 0.02),
        'moe_w_group': nrm((DEPTH, d, N_EXPERT_GROUPS), d ** -0.5),
        'moe_b_group': nrm((DEPTH, N_EXPERT_GROUPS), 0.01),
        'moe_w_expert': nrm((DEPTH, N_EXPERT_GROUPS, d, EXPERTS_PER_GROUP), d ** -0.5),
        'moe_b_expert': nrm((DEPTH, N_EXPERT_GROUPS, EXPERTS_PER_GROUP), 0.01),
        'moe_w_up': nrm((DEPTH, N_EXPERTS, d, 2 * EXPERT_FF), d ** -0.5),
        'moe_w_down': nrm((DEPTH, N_EXPERTS, EXPERT_FF, d), EXPERT_FF ** -0.5 * DEEPNORM_BETA),
    }


def reference(x_prompt, x_sample, cache_kv_w128, cache_kv_w512, cache_kv_w2048, state_pool,
              state_conv, a_w_in, a_w_out, b_w_in, b_w_group, b_scale, b_w_out, c_w_in, c_conv,
              c_w_out, ln1_g, ln1_b, ln2_g, ln2_b, moe_w_group, moe_b_group, moe_w_expert,
              moe_b_expert, moe_w_up, moe_w_down):
    params = dict(a_w_in=a_w_in, a_w_out=a_w_out, b_w_in=b_w_in, b_w_group=b_w_group,
                  b_scale=b_scale, b_w_out=b_w_out, c_w_in=c_w_in, c_conv=c_conv, c_w_out=c_w_out,
                  ln1_g=ln1_g, ln1_b=ln1_b, ln2_g=ln2_g, ln2_b=ln2_b, moe_w_group=moe_w_group,
                  moe_b_group=moe_b_group, moe_w_expert=moe_w_expert, moe_b_expert=moe_b_expert,
                  moe_w_up=moe_w_up, moe_w_down=moe_w_down)
    n_prompt = x_prompt.shape[0]
    zero_pool = jnp.zeros((N_B_LAYERS, n_prompt, POOL_STATE, D_MODEL), x_prompt.dtype)
    zero_conv = jnp.zeros((N_C_LAYERS, n_prompt, CONV_WIDTH - 1, D_MODEL), x_prompt.dtype)
    y_prompt, kv_p, pool_p, conv_p = trunk(x_prompt, 0, None, zero_pool, zero_conv, params)
    y_sample, kv_s, pool_s, conv_s = trunk(x_sample, PAST_LEN,
                                           (cache_kv_w128, cache_kv_w512, cache_kv_w2048),
                                           state_pool, state_conv, params)
    return (y_prompt, y_sample, kv_p[0], kv_s[0], kv_p[1], kv_s[1], kv_p[2], kv_s[2],
            pool_p, pool_s, conv_p, conv_s)
```

```python
import functools

import jax
import jax.numpy as jnp
from jax import lax
from jax.experimental import pallas as pl
from jax.experimental.pallas import tpu as pltpu

F32 = jnp.float32
BF16 = jnp.bfloat16

D_MODEL = 1024
BATCH = 8
SEQ = 2048
DEPTH = 4
DEC_BATCH = 128
PAST_LEN = 2048
N_MIXERS = 3
N_HEADS = 16
HEAD_DIM = 64
DIL_PAIRS = ((128, 1), (512, 4), (2048, 16))
N_DIL = 3
BAND = 128
POOL_WINDOWS = (2, 4, 8, 16)
POOL_CH = 256
POOL_STATE = 15
N_GROUPS = 4
EXP_PER_GROUP = 8
N_EXPERTS = 32
EXPERT_FF = 256
DEEPNORM_ALPHA = (2 * DEPTH) ** 0.25
LN_EPS = 1e-5
NEG_INF = -1e30
SCALE = HEAD_DIM ** -0.5

T_PROMPT = BATCH * SEQ
T_REAL = T_PROMPT + DEC_BATCH
TILE = 512
T_PAD = 16896
TAIL = T_PAD - T_PROMPT
LANES = 128
VMEM_LIMIT = 56 * 1024 * 1024


def _cparams(n_axes):
    return pltpu.CompilerParams(dimension_semantics=("arbitrary",) * n_axes,
                                vmem_limit_bytes=VMEM_LIMIT)


def _mm_kernel(x_ref, w_ref, o_ref, wb_ref):
    @pl.when(pl.program_id(1) == 0)
    def _():
        wb_ref[...] = w_ref[...].astype(BF16)

    o_ref[...] = jnp.dot(x_ref[...], wb_ref[...],
                         preferred_element_type=F32).astype(o_ref.dtype)


def matmul(x, w, layer, tn, tm, name):
    t, k = x.shape
    n = w.shape[-1]
    return pl.pallas_call(
        _mm_kernel,
        grid=(n // tn, t // tm),
        in_specs=[pl.BlockSpec((tm, k), lambda j, i: (i, 0)),
                  pl.BlockSpec((None, k, tn), lambda j, i: (layer, 0, j))],
        out_specs=pl.BlockSpec((tm, tn), lambda j, i: (i, j)),
        out_shape=jax.ShapeDtypeStruct((t, n), F32),
        scratch_shapes=[pltpu.VMEM((k, tn), BF16)],
        compiler_params=_cparams(2),
        name=name,
    )(x, w)


def _layer_norm(z, g, b):
    mu = jnp.mean(z, axis=-1, keepdims=True)
    zc = z - mu
    var = jnp.mean(zc * zc, axis=-1, keepdims=True)
    return zc * lax.rsqrt(var + LN_EPS) * g + b


def _route(logits):
    lane = lax.broadcasted_iota(jnp.int32, logits.shape, 1)
    gl = jnp.where(lane < N_GROUPS, logits, -jnp.inf)
    gmax = jnp.max(gl, axis=-1, keepdims=True)
    gsel = jnp.min(jnp.where(gl == gmax, lane, LANES), axis=-1, keepdims=True)
    p_g = 1.0 / jnp.sum(jnp.exp(gl - gmax), axis=-1, keepdims=True)
    lo = N_GROUPS + EXP_PER_GROUP * gsel
    el = jnp.where((lane >= lo) & (lane < lo + EXP_PER_GROUP), logits, -jnp.inf)
    v0 = jnp.max(el, axis=-1, keepdims=True)
    i0 = jnp.min(jnp.where(el == v0, lane, LANES), axis=-1, keepdims=True)
    el = jnp.where(lane == i0, -jnp.inf, el)
    v1 = jnp.max(el, axis=-1, keepdims=True)
    i1 = jnp.min(jnp.where(el == v1, lane, LANES), axis=-1, keepdims=True)
    e = jnp.exp(v1 - v0)
    q0 = 1.0 / (1.0 + e)
    q1 = e / (1.0 + e)
    rw = jnp.where(lane == 0, p_g * q0, jnp.where(lane == 1, p_g * q1, 0.0))
    ri = jnp.where(lane == 0, i0 - N_GROUPS, jnp.where(lane == 1, i1 - N_GROUPS, 0))
    return rw, ri


def _mix_out_kernel(a_ref, w_ref, x_ref, g_ref, b_ref, wr_ref, br_ref,
                    h_ref, hb_ref, rw_ref, ri_ref, wb_ref, wrb_ref):
    @pl.when(pl.program_id(0) == 0)
    def _():
        wb_ref[...] = w_ref[...].astype(BF16)
        wrb_ref[...] = wr_ref[...].astype(BF16)

    y = jnp.dot(a_ref[...], wb_ref[...], preferred_element_type=F32)
    h = _layer_norm(DEEPNORM_ALPHA * x_ref[...] + y, g_ref[...], b_ref[...])
    hb = h.astype(BF16)
    h_ref[...] = h
    hb_ref[...] = hb
    logits = jnp.dot(hb, wrb_ref[...], preferred_element_type=F32) + br_ref[...]
    rw, ri = _route(logits)
    rw_ref[...] = rw
    ri_ref[...] = ri


def mix_out(a, w_out, layer_w, x, g, b, wr, br, name):
    row = lambda i: (i, 0)
    fix = lambda i: (0, 0)
    return pl.pallas_call(
        _mix_out_kernel,
        grid=(T_PAD // TILE,),
        in_specs=[pl.BlockSpec((TILE, D_MODEL), row),
                  pl.BlockSpec((None, D_MODEL, D_MODEL), lambda i: (layer_w, 0, 0)),
                  pl.BlockSpec((TILE, D_MODEL), row),
                  pl.BlockSpec((1, D_MODEL), fix),
                  pl.BlockSpec((1, D_MODEL), fix),
                  pl.BlockSpec((D_MODEL, LANES), fix),
                  pl.BlockSpec((1, LANES), fix)],
        out_specs=[pl.BlockSpec((TILE, D_MODEL), row),
                   pl.BlockSpec((TILE, D_MODEL), row),
                   pl.BlockSpec((TILE, LANES), row),
                   pl.BlockSpec((TILE, LANES), row)],
        out_shape=[jax.ShapeDtypeStruct((T_PAD, D_MODEL), F32),
                   jax.ShapeDtypeStruct((T_PAD, D_MODEL), BF16),
                   jax.ShapeDtypeStruct((T_PAD, LANES), F32),
                   jax.ShapeDtypeStruct((T_PAD, LANES), jnp.int32)],
        scratch_shapes=[pltpu.VMEM((D_MODEL, D_MODEL), BF16),
                        pltpu.VMEM((D_MODEL, LANES), BF16)],
        compiler_params=_cparams(1),
        name=name,
    )(a, w_out, x, g, b, wr, br)


MOE_TILE = 1536


def _moe_kernel(hb_ref, rw_ref, ri_ref, wu_ref, wd_ref, f_ref):
    e = pl.program_id(1)

    @pl.when(e == 0)
    def _():
        f_ref[...] = jnp.zeros_like(f_ref)

    up = jnp.dot(hb_ref[...], wu_ref[...].astype(BF16), preferred_element_type=F32)
    a = up[:, :EXPERT_FF]
    u = up[:, EXPERT_FF:]
    gate = (jnp.where(ri_ref[:, 0:1] == e, rw_ref[:, 0:1], 0.0)
            + jnp.where(ri_ref[:, 1:2] == e, rw_ref[:, 1:2], 0.0))
    hid = (a / (1.0 + jnp.exp(-a))) * u * gate
    f_ref[...] += jnp.dot(hid.astype(BF16), wd_ref[...].astype(BF16),
                          preferred_element_type=F32)


def moe(hb, rw, ri, w_up, w_down, layer, name):
    row = lambda i, e: (i, 0)
    tm = MOE_TILE
    return pl.pallas_call(
        _moe_kernel,
        grid=(T_PAD // tm, N_EXPERTS),
        in_specs=[pl.BlockSpec((tm, D_MODEL), row),
                  pl.BlockSpec((tm, LANES), row),
                  pl.BlockSpec((tm, LANES), row),
                  pl.BlockSpec((None, None, D_MODEL, 2 * EXPERT_FF), lambda i, e: (layer, e, 0, 0)),
                  pl.BlockSpec((None, None, EXPERT_FF, D_MODEL), lambda i, e: (layer, e, 0, 0))],
        out_specs=pl.BlockSpec((tm, D_MODEL), row),
        out_shape=jax.ShapeDtypeStruct((T_PAD, D_MODEL), F32),
        compiler_params=_cparams(2),
        name=name,
    )(hb, rw, ri, w_up, w_down)


def _res_ln_kernel(h_ref, f_ref, g_ref, b_ref, x_ref, xb_ref):
    x = _layer_norm(DEEPNORM_ALPHA * h_ref[...] + f_ref[...], g_ref[...], b_ref[...])
    x_ref[...] = x
    xb_ref[...] = x.astype(BF16)


def res_ln(h, f, g, b, name):
    row = lambda i: (i, 0)
    fix = lambda i: (0, 0)
    return pl.pallas_call(
        _res_ln_kernel,
        grid=(T_PAD // TILE,),
        in_specs=[pl.BlockSpec((TILE, D_MODEL), row),
                  pl.BlockSpec((TILE, D_MODEL), row),
                  pl.BlockSpec((1, D_MODEL), fix),
                  pl.BlockSpec((1, D_MODEL), fix)],
        out_specs=[pl.BlockSpec((TILE, D_MODEL), row),
                   pl.BlockSpec((TILE, D_MODEL), row)],
        out_shape=[jax.ShapeDtypeStruct((T_PAD, D_MODEL), F32),
                   jax.ShapeDtypeStruct((T_PAD, D_MODEL), BF16)],
        compiler_params=_cparams(1),
        name=name,
    )(h, f, g, b)


def _softmax_pv(q, k, v, bias, valid):
    s = lax.dot_general(q, k, (((1,), (1,)), ((), ())), preferred_element_type=F32) * SCALE
    s = jnp.where(valid, s + bias, NEG_INF)
    m = jnp.max(s, axis=-1, keepdims=True)
    p = jnp.exp(s - m)
    den = jnp.sum(p, axis=-1, keepdims=True)
    pv = jnp.dot(p.astype(BF16), v, preferred_element_type=F32)
    return pv, m, den


def _attn_prompt_kernel(slopes_ref, q0, k0, v0, q1, k1, v1, q2, k2, v2, o_ref,
                        acc_o, acc_m, acc_d):
    hp = pl.program_id(1)
    q_refs, k_refs, v_refs = (q0, q1, q2), (k0, k1, k2), (v0, v1, v2)
    low = lax.broadcasted_iota(jnp.int32, (BAND, LANES), 1) < HEAD_DIM
    d2 = (lax.broadcasted_iota(jnp.int32, (BAND, 2 * BAND), 0) + BAND
          - lax.broadcasted_iota(jnp.int32, (BAND, 2 * BAND), 1))
    d1 = (lax.broadcasted_iota(jnp.int32, (BAND, BAND), 0)
          - lax.broadcasted_iota(jnp.int32, (BAND, BAND), 1))
    valid2 = (d2 >= 0) & (d2 <= BAND)
    valid1 = d1 >= 0
    d2f = d2.astype(F32)
    d1f = d1.astype(F32)
    slopes = (slopes_ref[2 * hp], slopes_ref[2 * hp + 1])

    for g, (_, dil) in enumerate(DIL_PAIRS):
        n_blk = SEQ // dil // BAND
        for r in range(dil):
            for b in range(n_blk):
                def rows(start, size):
                    return pl.ds(start, size, stride=dil) if dil > 1 else pl.ds(start, size)
                q_rows = rows(r + dil * BAND * b, BAND)
                if b == 0:
                    k_rows, distf, valid = rows(r, BAND), d1f, valid1
                else:
                    k_rows, distf, valid = rows(r + dil * BAND * (b - 1), 2 * BAND), d2f, valid2
                q = q_refs[g][q_rows, :]
                k = k_refs[g][k_rows, :].astype(BF16)
                v = v_refs[g][k_rows, :].astype(BF16)
                res = []
                for hh in range(2):
                    qm = jnp.where(low if hh == 0 else jnp.logical_not(low), q, 0.0).astype(BF16)
                    bias = distf * (-slopes[hh] * float(dil))
                    res.append(_softmax_pv(qm, k, v, bias, valid))
                acc_o[g, q_rows, :] = jnp.where(low, res[0][0], res[1][0])
                acc_m[g, q_rows, :] = jnp.where(low, res[0][1], res[1][1])
                acc_d[g, q_rows, :] = jnp.where(low, res[0][2], res[1][2])

    chunk = 256
    for c in range(SEQ // chunk):
        rws = pl.ds(c * chunk, chunk)
        m = [acc_m[g, rws, :] for g in range(N_DIL)]
        top = jnp.maximum(jnp.maximum(m[0], m[1]), m[2])
        num = jnp.zeros((chunk, LANES), F32)
        den = jnp.zeros((chunk, LANES), F32)
        for g in range(N_DIL):
            w = jnp.exp(m[g] - top)
            num = num + w * acc_o[g, rws, :]
            den = den + w * acc_d[g, rws, :]
        o_ref[rws, :] = (num / den).astype(o_ref.dtype)


def attn_prompt(proj, slopes):
    in_specs = [pl.BlockSpec(memory_space=pltpu.SMEM)]
    for g in range(N_DIL):
        for part in range(3):
            col0 = (g * 3 + part) * (D_MODEL // LANES)
            in_specs.append(pl.BlockSpec((SEQ, LANES),
                                         functools.partial(lambda n, hp, c: (n, c + hp), c=col0)))
    return pl.pallas_call(
        _attn_prompt_kernel,
        grid=(BATCH, N_HEADS // 2),
        in_specs=in_specs,
        out_specs=pl.BlockSpec((SEQ, LANES), lambda n, hp: (n, hp)),
        out_shape=jax.ShapeDtypeStruct((T_PAD, D_MODEL), BF16),
        scratch_shapes=[pltpu.VMEM((N_DIL, SEQ, LANES), F32)] * 3,
        compiler_params=_cparams(2),
        name="attn_prompt",
    )(slopes, *([proj] * 9))


SAMPLE_BN = 2


def _attn_sample_kernel(sl_ref, qkv_ref, c0_ref, c1_ref, c2_ref, o_ref):
    caches = (c0_ref, c1_ref, c2_ref)
    slope = sl_ref[...]
    back = (BAND - lax.broadcasted_iota(jnp.int32, (BAND, N_HEADS, 1), 0)).astype(F32)
    for n in range(SAMPLE_BN):
        ms, nums, dens = [], [], []
        for g, (_, dil) in enumerate(DIL_PAIRS):
            q = qkv_ref[n, g, 0]
            k_new = qkv_ref[n, g, 1]
            v_new = qkv_ref[n, g, 2]
            k_buf = caches[g][n, :, 0]
            v_buf = caches[g][n, :, 1]
            s_buf = jnp.sum(k_buf * q[None], axis=-1, keepdims=True) * SCALE
            s_buf = s_buf + (-slope)[None] * (back * float(dil))
            s_new = jnp.sum(k_new * q, axis=-1, keepdims=True) * SCALE
            m = jnp.maximum(jnp.max(s_buf, axis=0), s_new)
            p_buf = jnp.exp(s_buf - m[None])
            p_new = jnp.exp(s_new - m)
            dens.append(jnp.sum(p_buf, axis=0) + p_new)
            nums.append(jnp.sum(p_buf * v_buf, axis=0) + p_new * v_new)
            ms.append(m)
        top = jnp.maximum(jnp.maximum(ms[0], ms[1]), ms[2])
        num = jnp.zeros((N_HEADS, HEAD_DIM), F32)
        den = jnp.zeros((N_HEADS, 1), F32)
        for g in range(N_DIL):
            w = jnp.exp(ms[g] - top)
            num = num + w * nums[g]
            den = den + w * dens[g]
        o_ref[n] = num / den


def attn_sample(qkv, caches, layer, slopes_col):
    bn = SAMPLE_BN
    in_specs = [pl.BlockSpec((N_HEADS, 1), lambda i: (0, 0)),
                pl.BlockSpec((bn, N_DIL, 3, N_HEADS, HEAD_DIM), lambda i: (i, 0, 0, 0, 0))]
    views = []
    for cache, (win, dil) in zip(caches, DIL_PAIRS):
        views.append(cache.reshape(cache.shape[0], DEC_BATCH, win // dil, dil, 2, N_HEADS, HEAD_DIM))
        in_specs.append(pl.BlockSpec((None, bn, BAND, None, 2, N_HEADS, HEAD_DIM),
                                     lambda i: (layer, i, 0, 0, 0, 0, 0)))
    return pl.pallas_call(
        _attn_sample_kernel,
        grid=(DEC_BATCH // bn,),
        in_specs=in_specs,
        out_specs=pl.BlockSpec((bn, N_HEADS, HEAD_DIM), lambda i: (i, 0, 0)),
        out_shape=jax.ShapeDtypeStruct((DEC_BATCH, N_HEADS, HEAD_DIM), F32),
        compiler_params=_cparams(1),
        name="attn_sample",
    )(slopes_col, qkv, *views)


HIST = 16


def _pool_prompt_kernel(u_ref, wg_ref, sc_ref, z_ref, ext_ref, wgb_ref):
    i = pl.program_id(0)
    tiles_per_seq = SEQ // TILE

    @pl.when(i == 0)
    def _():
        wgb_ref[...] = wg_ref[...].astype(BF16)

    @pl.when(i % tiles_per_seq == 0)
    def _():
        ext_ref[0:HIST, :] = jnp.zeros((HIST, D_MODEL), F32)

    ext_ref[HIST:HIST + TILE, :] = u_ref[...]
    pos = (i % tiles_per_seq) * TILE + lax.broadcasted_iota(jnp.int32, (TILE, 1), 0)
    for g, win in enumerate(POOL_WINDOWS):
        cols = slice(g * POOL_CH, (g + 1) * POOL_CH)
        acc = ext_ref[HIST:HIST + TILE, cols]
        for k in range(1, win):
            acc = acc + ext_ref[HIST - k:HIST - k + TILE, cols]
        cnt = jnp.minimum(win, pos + 1).astype(F32)
        pooled = acc / cnt - u_ref[:, cols]
        z = jnp.dot(pooled.astype(BF16), wgb_ref[g], preferred_element_type=F32)
        z_ref[:, cols] = (z * sc_ref[:, cols]).astype(z_ref.dtype)
    ext_ref[0:HIST, :] = ext_ref[TILE:TILE + HIST, :]


def pool_prompt(u, w_group, scale):
    return pl.pallas_call(
        _pool_prompt_kernel,
        grid=(T_PROMPT // TILE,),
        in_specs=[pl.BlockSpec((TILE, D_MODEL), lambda i: (i, 0)),
                  pl.BlockSpec((None, len(POOL_WINDOWS), POOL_CH, POOL_CH), lambda i: (0, 0, 0, 0)),
                  pl.BlockSpec((1, D_MODEL), lambda i: (0, 0))],
        out_specs=pl.BlockSpec((TILE, D_MODEL), lambda i: (i, 0)),
        out_shape=jax.ShapeDtypeStruct((T_PAD, D_MODEL), BF16),
        scratch_shapes=[pltpu.VMEM((HIST + TILE, D_MODEL), F32),
                        pltpu.VMEM((len(POOL_WINDOWS), POOL_CH, POOL_CH), BF16)],
        compiler_params=_cparams(1),
        name="pool_prompt",
    )(u, w_group, scale)


def _pool_sample_kernel(u_ref, st_ref, wg_ref, sc_ref, z_ref):
    for g, win in enumerate(POOL_WINDOWS):
        cols = slice(g * POOL_CH, (g + 1) * POOL_CH)
        u = u_ref[:, cols]
        acc = u
        for k in range(1, win):
            acc = acc + st_ref[POOL_STATE - k, :, cols]
        cnt = float(min(win, PAST_LEN + 1))
        pooled = acc / cnt - u
        z = jnp.dot(pooled.astype(BF16), wg_ref[g].astype(BF16), preferred_element_type=F32)
        z_ref[:, cols] = (z * sc_ref[:, cols]).astype(z_ref.dtype)


def pool_sample(u, state_t, w_group, scale):
    return pl.pallas_call(
        _pool_sample_kernel,
        grid=(1,),
        in_specs=[pl.BlockSpec((DEC_BATCH, D_MODEL), lambda i: (T_PROMPT // DEC_BATCH, 0)),
                  pl.BlockSpec((POOL_STATE, DEC_BATCH, D_MODEL), lambda i: (0, 0, 0)),
                  pl.BlockSpec((None, len(POOL_WINDOWS), POOL_CH, POOL_CH), lambda i: (0, 0, 0, 0)),
                  pl.BlockSpec((1, D_MODEL), lambda i: (0, 0))],
        out_specs=pl.BlockSpec((DEC_BATCH, D_MODEL), lambda i: (0, 0)),
        out_shape=jax.ShapeDtypeStruct((DEC_BATCH, D_MODEL), BF16),
        compiler_params=_cparams(1),
        name="pool_sample",
    )(u, state_t, w_group, scale)


CONV_HIST = 8


def _conv_prompt_kernel(p_ref, cw_ref, y_ref, ch_ref, ext_ref):
    i = pl.program_id(0)

    @pl.when(i % (SEQ // TILE) == 0)
    def _():
        ext_ref[0:CONV_HIST, :] = jnp.zeros((CONV_HIST, D_MODEL), F32)

    ch = p_ref[:, D_MODEL:2 * D_MODEL] * p_ref[:, 2 * D_MODEL:]
    ext_ref[CONV_HIST:CONV_HIST + TILE, :] = ch
    conv = cw_ref[0:1, :] * ext_ref[CONV_HIST - 2:CONV_HIST - 2 + TILE, :]
    conv = conv + cw_ref[1:2, :] * ext_ref[CONV_HIST - 1:CONV_HIST - 1 + TILE, :]
    conv = conv + cw_ref[2:3, :] * ch
    y_ref[...] = (p_ref[:, :D_MODEL] * conv).astype(y_ref.dtype)
    ch_ref[...] = ch
    ext_ref[0:CONV_HIST, :] = ext_ref[TILE:TILE + CONV_HIST, :]


def conv_prompt(p, conv_w):
    return pl.pallas_call(
        _conv_prompt_kernel,
        grid=(T_PROMPT // TILE,),
        in_specs=[pl.BlockSpec((TILE, 3 * D_MODEL), lambda i: (i, 0)),
                  pl.BlockSpec((None, 3, D_MODEL), lambda i: (0, 0, 0))],
        out_specs=[pl.BlockSpec((TILE, D_MODEL), lambda i: (i, 0)),
                   pl.BlockSpec((TILE, D_MODEL), lambda i: (i, 0))],
        out_shape=[jax.ShapeDtypeStruct((T_PAD, D_MODEL), BF16),
                   jax.ShapeDtypeStruct((T_PROMPT, D_MODEL), F32)],
        scratch_shapes=[pltpu.VMEM((CONV_HIST + TILE, D_MODEL), F32)],
        compiler_params=_cparams(1),
        name="conv_prompt",
    )(p, conv_w)


def _conv_sample_kernel(p_ref, st_ref, cw_ref, y_ref, ch_ref):
    ch = p_ref[:, D_MODEL:2 * D_MODEL] * p_ref[:, 2 * D_MODEL:]
    conv = cw_ref[0:1, :] * st_ref[0]
    conv = conv + cw_ref[1:2, :] * st_ref[1]
    conv = conv + cw_ref[2:3, :] * ch
    y_ref[...] = (p_ref[:, :D_MODEL] * conv).astype(y_ref.dtype)
    ch_ref[...] = ch


def conv_sample(p, state_t, conv_w):
    return pl.pallas_call(
        _conv_sample_kernel,
        grid=(1,),
        in_specs=[pl.BlockSpec((DEC_BATCH, 3 * D_MODEL), lambda i: (T_PROMPT // DEC_BATCH, 0)),
                  pl.BlockSpec((2, DEC_BATCH, D_MODEL), lambda i: (0, 0, 0)),
                  pl.BlockSpec((None, 3, D_MODEL), lambda i: (0, 0, 0))],
        out_specs=[pl.BlockSpec((DEC_BATCH, D_MODEL), lambda i: (0, 0)),
                   pl.BlockSpec((DEC_BATCH, D_MODEL), lambda i: (0, 0))],
        out_shape=[jax.ShapeDtypeStruct((DEC_BATCH, D_MODEL), BF16),
                   jax.ShapeDtypeStruct((DEC_BATCH, D_MODEL), F32)],
        compiler_params=_cparams(1),
        name="conv_sample",
    )(p, state_t, conv_w)


def _with_tail(prompt_rows, sample_rows):
    tail = jnp.zeros((TAIL, prompt_rows.shape[1]), prompt_rows.dtype)
    tail = lax.dynamic_update_slice(tail, sample_rows.astype(prompt_rows.dtype), (0, 0))
    return lax.dynamic_update_slice(prompt_rows, tail, (T_PROMPT, 0))


def kernel(x_prompt, x_sample, cache_kv_w128, cache_kv_w512, cache_kv_w2048, state_pool, state_conv, a_w_in, a_w_out, b_w_in, b_w_group, b_scale, b_w_out, c_w_in, c_conv, c_w_out, ln1_g, ln1_b, ln2_g, ln2_b, moe_w_group, moe_b_group, moe_w_expert, moe_b_expert, moe_w_up, moe_w_down):
    caches = (cache_kv_w128, cache_kv_w512, cache_kv_w2048)
    heads = jnp.arange(1, N_HEADS + 1, dtype=F32)
    slopes = jnp.exp2(-8.0 * heads / N_HEADS)

    x = jnp.concatenate([x_prompt.reshape(T_PROMPT, D_MODEL),
                         x_sample.reshape(DEC_BATCH, D_MODEL),
                         jnp.zeros((T_PAD - T_REAL, D_MODEL), F32)], axis=0)
    xb = x.astype(BF16)

    kv_p = [[] for _ in DIL_PAIRS]
    kv_s = [[] for _ in DIL_PAIRS]
    pool_p = pool_s = conv_p = conv_s = None

    for i in range(DEPTH):
        j, kind = i // N_MIXERS, i % N_MIXERS
        if kind == 0:
            proj = matmul(xb, a_w_in, j, tn=1536, tm=1536, name=f"a_in_{i}")
            o = attn_prompt(proj, slopes)
            proj_s = proj[T_PROMPT:T_REAL]
            qkv_s = proj_s.reshape(DEC_BATCH, N_DIL, 3, N_HEADS, HEAD_DIM)
            o_s = attn_sample(qkv_s, caches, j, slopes.reshape(N_HEADS, 1))
            mixed = _with_tail(o, o_s.reshape(DEC_BATCH, D_MODEL))
            w_out, layer_w = a_w_out, j
            proj_p = proj[:T_PROMPT].reshape(BATCH, SEQ, N_DIL, 3, N_HEADS, HEAD_DIM)
            for g, (win, _) in enumerate(DIL_PAIRS):
                kv_p[g].append(proj_p[:, SEQ - min(win, SEQ):, g, 1:3])
                kv_s[g].append(qkv_s[:, None, g, 1:3])
        elif kind == 1:
            u = matmul(xb, b_w_in, j, tn=1024, tm=1536, name=f"b_in_{i}")
            z = pool_prompt(u, b_w_group[j:j + 1], b_scale[j:j + 1])
            state_t = jnp.swapaxes(state_pool[j], 0, 1)
            z_s = pool_sample(u, state_t, b_w_group[j:j + 1], b_scale[j:j + 1])
            mixed = _with_tail(z, z_s)
            w_out, layer_w = b_w_out, j
            u_p = u[:T_PROMPT].reshape(BATCH, SEQ, D_MODEL)
            u_s = u[T_PROMPT:T_REAL]
            pool_p = u_p[:, SEQ - POOL_STATE:][None]
            pool_s = jnp.concatenate([state_pool[j][:, 1:], u_s[:, None]], axis=1)[None]
        else:
            p3 = matmul(xb, c_w_in, j, tn=1536, tm=1536, name=f"c_in_{i}")
            y, ch = conv_prompt(p3, c_conv[j:j + 1])
            state_t = jnp.swapaxes(state_conv[j], 0, 1)
            y_s, ch_s = conv_sample(p3, state_t, c_conv[j:j + 1])
            mixed = _with_tail(y, y_s)
            w_out, layer_w = c_w_out, j
            conv_p = ch.reshape(BATCH, SEQ, D_MODEL)[:, SEQ - 2:][None]
            conv_s = jnp.concatenate([state_conv[j][:, 1:], ch_s[:, None]], axis=1)[None]

        wr = jnp.concatenate(
            [moe_w_group[i],
             jnp.transpose(moe_w_expert[i], (1, 0, 2)).reshape(D_MODEL, N_EXPERTS),
             jnp.zeros((D_MODEL, LANES - N_GROUPS - N_EXPERTS), F32)], axis=1)
        br = jnp.concatenate([moe_b_group[i], moe_b_expert[i].reshape(N_EXPERTS),
                              jnp.zeros((LANES - N_GROUPS - N_EXPERTS,), F32)])[None]
        h, hb, rw, ri = mix_out(mixed, w_out, layer_w, x, ln1_g[i:i + 1], ln1_b[i:i + 1],
                                wr, br, name=f"mix_out_{i}")
        f = moe(hb, rw, ri, moe_w_up, moe_w_down, i, name=f"moe_{i}")
        x, xb = res_ln(h, f, ln2_g[i:i + 1], ln2_b[i:i + 1], name=f"res_ln_{i}")

    y_prompt = x[:T_PROMPT].reshape(BATCH, SEQ, D_MODEL)
    y_sample = x[T_PROMPT:T_REAL].reshape(DEC_BATCH, 1, D_MODEL)
    stack = lambda rows: jnp.stack(rows)
    return (y_prompt, y_sample,
            stack(kv_p[0]), stack(kv_s[0]), stack(kv_p[1]), stack(kv_s[1]),
            stack(kv_p[2]), stack(kv_s[2]), pool_p, pool_s, conv_p, conv_s)
```

```python
import functools

import jax
import jax.numpy as jnp
from jax import lax
from jax.experimental import pallas as pl
from jax.experimental.pallas import tpu as pltpu

F32 = jnp.float32
BF16 = jnp.bfloat16

D_MODEL = 1024
BATCH = 8
SEQ = 2048
DEPTH = 4
DEC_BATCH = 128
PAST_LEN = 2048
N_MIXERS = 3
N_HEADS = 16
HEAD_DIM = 64
DIL_PAIRS = ((128, 1), (512, 4), (2048, 16))
N_DIL = 3
BAND = 128
POOL_WINDOWS = (2, 4, 8, 16)
POOL_CH = 256
POOL_STATE = 15
N_GROUPS = 4
EXP_PER_GROUP = 8
N_EXPERTS = 32
EXPERT_FF = 256
DEEPNORM_ALPHA = (2 * DEPTH) ** 0.25
LN_EPS = 1e-5
NEG_INF = -1e30
SCALE = HEAD_DIM ** -0.5

T_PROMPT = BATCH * SEQ
T_REAL = T_PROMPT + DEC_BATCH
TILE = 512
T_PAD = 16896
TAIL = T_PAD - T_PROMPT
LANES = 128
VMEM_LIMIT = 56 * 1024 * 1024


def _cparams(n_axes):
    return pltpu.CompilerParams(dimension_semantics=("arbitrary",) * n_axes,
                                vmem_limit_bytes=VMEM_LIMIT)


def _mm_kernel(x_ref, w_ref, o_ref, wb_ref):
    @pl.when(pl.program_id(1) == 0)
    def _():
        wb_ref[...] = w_ref[...].astype(BF16)

    o_ref[...] = jnp.dot(x_ref[...], wb_ref[...],
                         preferred_element_type=F32).astype(o_ref.dtype)


def matmul(x, w, layer, row0, rows, tn, tm, name):
    k = x.shape[1]
    n = w.shape[-1]
    blk0 = row0 // tm
    return pl.pallas_call(
        _mm_kernel,
        grid=(n // tn, rows // tm),
        in_specs=[pl.BlockSpec((tm, k), lambda j, i: (blk0 + i, 0)),
                  pl.BlockSpec((None, k, tn), lambda j, i: (layer, 0, j))],
        out_specs=pl.BlockSpec((tm, tn), lambda j, i: (i, j)),
        out_shape=jax.ShapeDtypeStruct((rows, n), F32),
        scratch_shapes=[pltpu.VMEM((k, tn), BF16)],
        compiler_params=_cparams(2),
        name=name,
    )(x, w)


def project(xb, w, layer, tn_tail, name):
    prompt = matmul(xb, w, layer, 0, T_PROMPT, 1024, SEQ, name + "_p")
    tail = matmul(xb, w, layer, T_PROMPT, TAIL, tn_tail, TAIL, name + "_t")
    return prompt, tail


def _layer_norm(z, g, b):
    mu = jnp.mean(z, axis=-1, keepdims=True)
    zc = z - mu
    var = jnp.mean(zc * zc, axis=-1, keepdims=True)
    return zc * lax.rsqrt(var + LN_EPS) * g + b


def _route(logits):
    lane = lax.broadcasted_iota(jnp.int32, logits.shape, 1)
    gl = jnp.where(lane < N_GROUPS, logits, -jnp.inf)
    gmax = jnp.max(gl, axis=-1, keepdims=True)
    gsel = jnp.min(jnp.where(gl == gmax, lane, LANES), axis=-1, keepdims=True)
    p_g = 1.0 / jnp.sum(jnp.exp(gl - gmax), axis=-1, keepdims=True)
    lo = N_GROUPS + EXP_PER_GROUP * gsel
    el = jnp.where((lane >= lo) & (lane < lo + EXP_PER_GROUP), logits, -jnp.inf)
    v0 = jnp.max(el, axis=-1, keepdims=True)
    i0 = jnp.min(jnp.where(el == v0, lane, LANES), axis=-1, keepdims=True)
    el = jnp.where(lane == i0, -jnp.inf, el)
    v1 = jnp.max(el, axis=-1, keepdims=True)
    i1 = jnp.min(jnp.where(el == v1, lane, LANES), axis=-1, keepdims=True)
    e = jnp.exp(v1 - v0)
    q0 = 1.0 / (1.0 + e)
    q1 = e / (1.0 + e)
    rw = jnp.where(lane == 0, p_g * q0, jnp.where(lane == 1, p_g * q1, 0.0))
    ri = jnp.where(lane == 0, i0 - N_GROUPS, jnp.where(lane == 1, i1 - N_GROUPS, 0))
    return rw, ri


def _mix_out_kernel(a_ref, at_ref, w_ref, x_ref, g_ref, b_ref, wr_ref, br_ref,
                    h_ref, hb_ref, rw_ref, ri_ref, wb_ref, wrb_ref):
    i = pl.program_id(0)

    @pl.when(i == 0)
    def _():
        wb_ref[...] = w_ref[...].astype(BF16)
        wr = wr_ref[...]
        wr_hi = wr.astype(BF16)
        wrb_ref[:, :LANES] = wr_hi
        wrb_ref[:, LANES:] = (wr - wr_hi.astype(F32)).astype(BF16)

    a = jnp.where(i < T_PROMPT // TILE, a_ref[...], at_ref[...])
    y = jnp.dot(a, wb_ref[...], preferred_element_type=F32)
    h = _layer_norm(DEEPNORM_ALPHA * x_ref[...] + y, g_ref[...], b_ref[...])
    hb = h.astype(BF16)
    h_ref[...] = h
    hb_ref[...] = hb
    h_lo = (h - hb.astype(F32)).astype(BF16)
    split = jnp.dot(hb, wrb_ref[...], preferred_element_type=F32)
    logits = (split[:, :LANES] + split[:, LANES:]
              + jnp.dot(h_lo, wrb_ref[:, :LANES], preferred_element_type=F32) + br_ref[...])
    rw, ri = _route(logits)
    rw_ref[...] = rw
    ri_ref[...] = ri


def mix_out(a, a_tail, w_out, layer_w, x, g, b, wr, br, name):
    row = lambda i: (i, 0)
    fix = lambda i: (0, 0)
    return pl.pallas_call(
        _mix_out_kernel,
        grid=(T_PAD // TILE,),
        in_specs=[pl.BlockSpec((TILE, D_MODEL), lambda i: (jnp.minimum(i, T_PROMPT // TILE - 1), 0)),
                  pl.BlockSpec((TAIL, D_MODEL), fix),
                  pl.BlockSpec((None, D_MODEL, D_MODEL), lambda i: (layer_w, 0, 0)),
                  pl.BlockSpec((TILE, D_MODEL), row),
                  pl.BlockSpec((1, D_MODEL), fix),
                  pl.BlockSpec((1, D_MODEL), fix),
                  pl.BlockSpec((D_MODEL, LANES), fix),
                  pl.BlockSpec((1, LANES), fix)],
        out_specs=[pl.BlockSpec((TILE, D_MODEL), row),
                   pl.BlockSpec((TILE, D_MODEL), row),
                   pl.BlockSpec((TILE, LANES), row),
                   pl.BlockSpec((TILE, LANES), row)],
        out_shape=[jax.ShapeDtypeStruct((T_PAD, D_MODEL), F32),
                   jax.ShapeDtypeStruct((T_PAD, D_MODEL), BF16),
                   jax.ShapeDtypeStruct((T_PAD, LANES), F32),
                   jax.ShapeDtypeStruct((T_PAD, LANES), jnp.int32)],
        scratch_shapes=[pltpu.VMEM((D_MODEL, D_MODEL), BF16),
                        pltpu.VMEM((D_MODEL, 2 * LANES), BF16)],
        compiler_params=_cparams(1),
        name=name,
    )(a, a_tail, w_out, x, g, b, wr, br)


EXPERT_TILE = 512
N_ASSIGN = 2 * T_PAD
MAX_EXPERT_TILES = (N_ASSIGN + N_EXPERTS * (EXPERT_TILE - 1)) // EXPERT_TILE
SORTED_ROWS = MAX_EXPERT_TILES * EXPERT_TILE


def _plan_kernel(ri_ref, rank_ref, cnt_ref, base_ref):
    i = pl.program_id(0)

    @pl.when(i == 0)
    def _():
        base_ref[...] = jnp.zeros_like(base_ref)

    lane = lax.broadcasted_iota(jnp.int32, (TILE, LANES), 1)
    e0 = ri_ref[:, 0:1]
    e1 = ri_ref[:, 1:2]
    hit0 = lane == e0
    hit1 = lane == e1
    onehot = jnp.where(hit0 | hit1, 1.0, 0.0)
    r = lax.broadcasted_iota(jnp.int32, (TILE, TILE), 0)
    c = lax.broadcasted_iota(jnp.int32, (TILE, TILE), 1)
    earlier = jnp.where(c < r, 1.0, 0.0).astype(BF16)
    before = jnp.dot(earlier, onehot.astype(BF16), preferred_element_type=F32) + base_ref[...]
    rank0 = jnp.sum(jnp.where(hit0, before, 0.0), axis=-1, keepdims=True)
    rank1 = jnp.sum(jnp.where(hit1, before, 0.0), axis=-1, keepdims=True)
    rank_ref[...] = jnp.where(lane == 0, rank0, jnp.where(lane == 1, rank1, 0.0)).astype(jnp.int32)
    base_ref[...] = base_ref[...] + jnp.sum(onehot, axis=0, keepdims=True)
    cnt_ref[...] = jnp.broadcast_to(base_ref[...], cnt_ref.shape).astype(jnp.int32)


def moe_plan(ri, name):
    return pl.pallas_call(
        _plan_kernel,
        grid=(T_PAD // TILE,),
        in_specs=[pl.BlockSpec((TILE, LANES), lambda i: (i, 0))],
        out_specs=[pl.BlockSpec((TILE, LANES), lambda i: (i, 0)),
                   pl.BlockSpec((8, LANES), lambda i: (0, 0))],
        out_shape=[jax.ShapeDtypeStruct((T_PAD, LANES), jnp.int32),
                   jax.ShapeDtypeStruct((8, LANES), jnp.int32)],
        scratch_shapes=[pltpu.VMEM((1, LANES), F32)],
        compiler_params=_cparams(1),
        name=name,
    )(ri)


def _row_copy(src_hbm, dst_hbm, src_row, dst_row, sem):
    return pltpu.make_async_copy(src_hbm.at[pl.ds(src_row, 1)], dst_hbm.at[pl.ds(dst_row, 1)], sem)


SUBLANES = 8
FILL_PIECES = tuple(2 ** k for k in range(EXPERT_TILE.bit_length() - 2, 2, -1))


def _zero_fill(fill_start_ref, fill_len_ref, n_tiles_ref, zeros_ref, xs_hbm, sem, wait):
    def run(copy):
        copy.wait() if wait else copy.start()

    for e in range(N_EXPERTS):
        start = fill_start_ref[e]
        head = (-start) & (SUBLANES - 1)
        for k in range(SUBLANES - 1):
            @pl.when(k < head)
            def _(k=k):
                run(pltpu.make_async_copy(zeros_ref.at[pl.ds(0, 1)], xs_hbm.at[pl.ds(start + k, 1)], sem))
        row = start + head
        length = fill_len_ref[e] - head
        for piece in FILL_PIECES:
            @pl.when((length & piece) != 0)
            def _(row=row, piece=piece):
                run(pltpu.make_async_copy(zeros_ref.at[pl.ds(0, piece)],
                                          xs_hbm.at[pl.ds(pl.multiple_of(row, SUBLANES), piece)], sem))
            row = row + (length & piece)

    def tail(t, carry):
        run(pltpu.make_async_copy(zeros_ref, xs_hbm.at[pl.ds(t * EXPERT_TILE, EXPERT_TILE)], sem))
        return carry

    lax.fori_loop(n_tiles_ref[0], MAX_EXPERT_TILES, tail, 0)


def _dispatch_kernel(pos_ref, fill_start_ref, fill_len_ref, n_tiles_ref, h_hbm, xs_hbm,
                     zeros_ref, sem, fill_sem):
    i = pl.program_id(0)
    base = i * TILE

    @pl.when(i == 0)
    def _():
        zeros_ref[...] = jnp.zeros_like(zeros_ref)
        _zero_fill(fill_start_ref, fill_len_ref, n_tiles_ref, zeros_ref, xs_hbm, fill_sem, wait=False)

    def issue(r, carry):
        _row_copy(h_hbm, xs_hbm, base + r, pos_ref[0, 2 * r], sem).start()
        _row_copy(h_hbm, xs_hbm, base + r, pos_ref[0, 2 * r + 1], sem).start()
        return carry

    lax.fori_loop(0, TILE, issue, 0)

    def drain(r, carry):
        _row_copy(h_hbm, xs_hbm, base + r, pos_ref[0, 2 * r], sem).wait()
        _row_copy(h_hbm, xs_hbm, base + r, pos_ref[0, 2 * r + 1], sem).wait()
        return carry

    lax.fori_loop(0, TILE, drain, 0)

    @pl.when(i == 0)
    def _():
        _zero_fill(fill_start_ref, fill_len_ref, n_tiles_ref, zeros_ref, xs_hbm, fill_sem, wait=True)


def moe_dispatch(pos, fill_start, fill_len, n_tiles, h, name):
    smem = pl.BlockSpec(memory_space=pltpu.SMEM)
    return pl.pallas_call(
        _dispatch_kernel,
        grid=(T_PAD // TILE,),
        in_specs=[pl.BlockSpec((None, 1, 2 * TILE), lambda i: (i, 0, 0), memory_space=pltpu.SMEM),
                  smem, smem, smem,
                  pl.BlockSpec(memory_space=pl.ANY)],
        out_specs=pl.BlockSpec(memory_space=pl.ANY),
        out_shape=jax.ShapeDtypeStruct((SORTED_ROWS, D_MODEL), F32),
        scratch_shapes=[pltpu.VMEM((EXPERT_TILE, D_MODEL), F32),
                        pltpu.SemaphoreType.DMA(()),
                        pltpu.SemaphoreType.DMA(())],
        compiler_params=pltpu.CompilerParams(dimension_semantics=("arbitrary",),
                                             has_side_effects=True),
        name=name,
    )(pos, fill_start, fill_len, n_tiles, h)


def _expert_kernel(tile_expert_ref, tile_first_ref, n_tiles_ref, xs_ref, wu_ref, wd_ref, ys_ref,
                   wub_ref, wdb_ref):
    t = pl.program_id(0)

    @pl.when(t < n_tiles_ref[0])
    def _():
        @pl.when(tile_first_ref[t] == 1)
        def _():
            wub_ref[...] = wu_ref[...].astype(BF16)
            wdb_ref[...] = wd_ref[...].astype(BF16)

        up = jnp.dot(xs_ref[...].astype(BF16), wub_ref[...], preferred_element_type=F32)
        a = up[:, :EXPERT_FF]
        u = up[:, EXPERT_FF:]
        hid = (a / (1.0 + jnp.exp(-a))) * u
        ys_ref[...] = jnp.dot(hid.astype(BF16), wdb_ref[...], preferred_element_type=F32)

    @pl.when(t >= n_tiles_ref[0])
    def _():
        ys_ref[...] = jnp.zeros_like(ys_ref)


def moe_experts(tile_expert, tile_first, n_tiles, xs, w_up, w_down, layer, name):
    def row_map(t, te, tf, nt):
        return (jnp.minimum(t, nt[0] - 1), 0)

    def out_map(t, te, tf, nt):
        return (t, 0)

    def w_map(t, te, tf, nt):
        return (layer, te[jnp.minimum(t, nt[0] - 1)], 0, 0)

    grid_spec = pltpu.PrefetchScalarGridSpec(
        num_scalar_prefetch=3,
        grid=(MAX_EXPERT_TILES,),
        in_specs=[pl.BlockSpec((EXPERT_TILE, D_MODEL), row_map),
                  pl.BlockSpec((None, None, D_MODEL, 2 * EXPERT_FF), w_map),
                  pl.BlockSpec((None, None, EXPERT_FF, D_MODEL), w_map)],
        out_specs=pl.BlockSpec((EXPERT_TILE, D_MODEL), out_map),
        scratch_shapes=[pltpu.VMEM((D_MODEL, 2 * EXPERT_FF), BF16),
                        pltpu.VMEM((EXPERT_FF, D_MODEL), BF16)])
    return pl.pallas_call(
        _expert_kernel,
        grid_spec=grid_spec,
        out_shape=jax.ShapeDtypeStruct((SORTED_ROWS, D_MODEL), F32),
        compiler_params=_cparams(1),
        name=name,
    )(tile_expert, tile_first, n_tiles, xs, w_up, w_down)


def _combine_kernel(pos_ref, h_ref, rw_ref, g_ref, b_ref, ys_hbm, x_ref, xb_ref, buf0, buf1, sem):
    def issue(r, carry):
        pltpu.make_async_copy(ys_hbm.at[pl.ds(pos_ref[0, 2 * r], 1)], buf0.at[pl.ds(r, 1)], sem).start()
        pltpu.make_async_copy(ys_hbm.at[pl.ds(pos_ref[0, 2 * r + 1], 1)], buf1.at[pl.ds(r, 1)], sem).start()
        return carry

    lax.fori_loop(0, TILE, issue, 0)

    def drain(r, carry):
        pltpu.make_async_copy(ys_hbm.at[pl.ds(pos_ref[0, 2 * r], 1)], buf0.at[pl.ds(r, 1)], sem).wait()
        pltpu.make_async_copy(ys_hbm.at[pl.ds(pos_ref[0, 2 * r + 1], 1)], buf1.at[pl.ds(r, 1)], sem).wait()
        return carry

    lax.fori_loop(0, TILE, drain, 0)
    f = rw_ref[:, 0:1] * buf0[...] + rw_ref[:, 1:2] * buf1[...]
    x = _layer_norm(DEEPNORM_ALPHA * h_ref[...] + f, g_ref[...], b_ref[...])
    x_ref[...] = x
    xb_ref[...] = x.astype(BF16)


def moe_combine(pos, h, rw, g, b, ys, name):
    row = lambda i: (i, 0)
    fix = lambda i: (0, 0)
    return pl.pallas_call(
        _combine_kernel,
        grid=(T_PAD // TILE,),
        in_specs=[pl.BlockSpec((None, 1, 2 * TILE), lambda i: (i, 0, 0), memory_space=pltpu.SMEM),
                  pl.BlockSpec((TILE, D_MODEL), row),
                  pl.BlockSpec((TILE, LANES), row),
                  pl.BlockSpec((1, D_MODEL), fix),
                  pl.BlockSpec((1, D_MODEL), fix),
                  pl.BlockSpec(memory_space=pl.ANY)],
        out_specs=[pl.BlockSpec((TILE, D_MODEL), row),
                   pl.BlockSpec((TILE, D_MODEL), row)],
        out_shape=[jax.ShapeDtypeStruct((T_PAD, D_MODEL), F32),
                   jax.ShapeDtypeStruct((T_PAD, D_MODEL), BF16)],
        scratch_shapes=[pltpu.VMEM((TILE, D_MODEL), F32),
                        pltpu.VMEM((TILE, D_MODEL), F32),
                        pltpu.SemaphoreType.DMA(())],
        compiler_params=_cparams(1),
        name=name,
    )(pos, h, rw, g, b, ys)


def sparse_moe(h, rw, ri, w_up, w_down, layer, g, b, tag):
    rank, cnt = moe_plan(ri, name=f"moe_plan_{tag}")
    counts = cnt[0, :N_EXPERTS]
    tiles = (counts + EXPERT_TILE - 1) // EXPERT_TILE
    tile_end = jnp.cumsum(tiles)
    tile_start = tile_end - tiles
    row_start = tile_start * EXPERT_TILE
    pos = jnp.take(row_start, ri[:, 0:2], axis=0) + rank[:, 0:2]
    pos = pos.reshape(T_PAD // TILE, 1, 2 * TILE)
    tile_ids = jnp.arange(MAX_EXPERT_TILES, dtype=jnp.int32)
    tile_expert = jnp.minimum(jnp.sum(tile_ids[:, None] >= tile_end[None, :], axis=1),
                              N_EXPERTS - 1).astype(jnp.int32)
    tile_first = jnp.any(tile_ids[:, None] == tile_start[None, :], axis=1).astype(jnp.int32)
    n_tiles = tile_end[-1:].astype(jnp.int32)
    fill_start = (row_start + counts).astype(jnp.int32)
    fill_len = (tiles * EXPERT_TILE - counts).astype(jnp.int32)
    xs = moe_dispatch(pos, fill_start, fill_len, n_tiles, h, name=f"moe_dispatch_{tag}")
    ys = moe_experts(tile_expert, tile_first, n_tiles, xs, w_up, w_down, layer, name=f"moe_experts_{tag}")
    return moe_combine(pos, h, rw, g, b, ys, name=f"moe_combine_{tag}")


def _softmax_pv(q, k, v, bias, valid):
    s = lax.dot_general(q, k, (((1,), (1,)), ((), ())), preferred_element_type=F32) * SCALE
    s = jnp.where(valid, s + bias, NEG_INF)
    m = jnp.max(s, axis=-1, keepdims=True)
    p = jnp.exp(s - m)
    den = jnp.sum(p, axis=-1, keepdims=True)
    pv = jnp.dot(p.astype(BF16), v, preferred_element_type=F32)
    return pv, m, den


def _attn_prompt_kernel(slopes_ref, q0, k0, v0, q1, k1, v1, q2, k2, v2, o_ref,
                        acc_o, acc_m, acc_d):
    hp = pl.program_id(1)
    q_refs, k_refs, v_refs = (q0, q1, q2), (k0, k1, k2), (v0, v1, v2)
    low = lax.broadcasted_iota(jnp.int32, (BAND, LANES), 1) < HEAD_DIM
    d2 = (lax.broadcasted_iota(jnp.int32, (BAND, 2 * BAND), 0) + BAND
          - lax.broadcasted_iota(jnp.int32, (BAND, 2 * BAND), 1))
    d1 = (lax.broadcasted_iota(jnp.int32, (BAND, BAND), 0)
          - lax.broadcasted_iota(jnp.int32, (BAND, BAND), 1))
    valid2 = (d2 >= 0) & (d2 <= BAND)
    valid1 = d1 >= 0
    d2f = d2.astype(F32)
    d1f = d1.astype(F32)
    slopes = (slopes_ref[2 * hp], slopes_ref[2 * hp + 1])

    for g, (_, dil) in enumerate(DIL_PAIRS):
        n_blk = SEQ // dil // BAND
        for r in range(dil):
            for b in range(n_blk):
                def rows(start, size):
                    return pl.ds(start, size, stride=dil) if dil > 1 else pl.ds(start, size)
                q_rows = rows(r + dil * BAND * b, BAND)
                if b == 0:
                    k_rows, distf, valid = rows(r, BAND), d1f, valid1
                else:
                    k_rows, distf, valid = rows(r + dil * BAND * (b - 1), 2 * BAND), d2f, valid2
                q = q_refs[g][q_rows, :]
                k = k_refs[g][k_rows, :].astype(BF16)
                v = v_refs[g][k_rows, :].astype(BF16)
                res = []
                for hh in range(2):
                    qm = jnp.where(low if hh == 0 else jnp.logical_not(low), q, 0.0).astype(BF16)
                    bias = distf * (-slopes[hh] * float(dil))
                    res.append(_softmax_pv(qm, k, v, bias, valid))
                acc_o[g, q_rows, :] = jnp.where(low, res[0][0], res[1][0])
                acc_m[g, q_rows, :] = jnp.where(low, res[0][1], res[1][1])
                acc_d[g, q_rows, :] = jnp.where(low, res[0][2], res[1][2])

    chunk = 256
    for c in range(SEQ // chunk):
        rws = pl.ds(c * chunk, chunk)
        m = [acc_m[g, rws, :] for g in range(N_DIL)]
        top = jnp.maximum(jnp.maximum(m[0], m[1]), m[2])
        num = jnp.zeros((chunk, LANES), F32)
        den = jnp.zeros((chunk, LANES), F32)
        for g in range(N_DIL):
            w = jnp.exp(m[g] - top)
            num = num + w * acc_o[g, rws, :]
            den = den + w * acc_d[g, rws, :]
        o_ref[rws, :] = (num / den).astype(o_ref.dtype)


def attn_prompt(proj, slopes):
    in_specs = [pl.BlockSpec(memory_space=pltpu.SMEM)]
    for g in range(N_DIL):
        for part in range(3):
            col0 = (g * 3 + part) * (D_MODEL // LANES)
            in_specs.append(pl.BlockSpec((SEQ, LANES),
                                         functools.partial(lambda n, hp, c: (n, c + hp), c=col0)))
    return pl.pallas_call(
        _attn_prompt_kernel,
        grid=(BATCH, N_HEADS // 2),
        in_specs=in_specs,
        out_specs=pl.BlockSpec((SEQ, LANES), lambda n, hp: (n, hp)),
        out_shape=jax.ShapeDtypeStruct((T_PROMPT, D_MODEL), BF16),
        scratch_shapes=[pltpu.VMEM((N_DIL, SEQ, LANES), F32)] * 3,
        compiler_params=_cparams(2),
        name="attn_prompt",
    )(slopes, *([proj] * 9))


SAMPLE_HB = 8
SAMPLE_CHUNK = 512


def _attn_sample_kernel(slopes_ref, qkv_ref, c0_ref, c1_ref, c2_ref, o_ref):
    hb = pl.program_id(0)
    n = pl.program_id(1)
    caches = (c0_ref, c1_ref, c2_ref)

    @pl.when(n == 0)
    def _():
        o_ref[...] = jnp.zeros_like(o_ref)

    is_n = lax.broadcasted_iota(jnp.int32, (HEAD_DIM, DEC_BATCH), 1) == n

    def column(g, part, h):
        return jnp.sum(jnp.where(is_n, qkv_ref[g, part, h], 0.0), axis=1, keepdims=True)

    for h in range(SAMPLE_HB):
        neg_slope = -slopes_ref[hb * SAMPLE_HB + h]
        ms, nums, dens = [], [], []
        for g, (win, dil) in enumerate(DIL_PAIRS):
            q = column(g, 0, h)
            k_new = column(g, 1, h)
            v_new = column(g, 2, h)
            chunk = min(win, SAMPLE_CHUNK)
            s_new = jnp.sum(q * k_new, axis=0, keepdims=True) * SCALE
            scores = []
            m = s_new
            for c in range(win // chunk):
                k_t = caches[g][0, h, :, c * chunk:(c + 1) * chunk]
                s = jnp.sum(k_t * q, axis=0, keepdims=True) * SCALE
                pos = c * chunk + lax.broadcasted_iota(jnp.int32, (1, chunk), 1)
                s = s + (win - pos).astype(F32) * neg_slope
                if dil > 1:
                    s = jnp.where((pos & (dil - 1)) == 0, s, NEG_INF)
                scores.append(s)
                m = jnp.maximum(m, jnp.max(s, axis=1, keepdims=True))
            p_new = jnp.exp(s_new - m)
            den = p_new
            acc = jnp.zeros((HEAD_DIM, chunk), F32)
            for c, s in enumerate(scores):
                p = jnp.exp(s - m)
                den = den + jnp.sum(p, axis=1, keepdims=True)
                acc = acc + caches[g][1, h, :, c * chunk:(c + 1) * chunk] * p
            nums.append(jnp.sum(acc, axis=1, keepdims=True) + p_new * v_new)
            dens.append(den)
            ms.append(m)
        top = jnp.maximum(jnp.maximum(ms[0], ms[1]), ms[2])
        num = jnp.zeros((HEAD_DIM, 1), F32)
        den = jnp.zeros((1, 1), F32)
        for g in range(N_DIL):
            w = jnp.exp(ms[g] - top)
            num = num + w * nums[g]
            den = den + w * dens[g]
        o_ref[h] = jnp.where(is_n, num / den, o_ref[h])


def attn_sample(qkv_t, caches, layer, slopes):
    hb = SAMPLE_HB
    in_specs = [pl.BlockSpec(memory_space=pltpu.SMEM),
                pl.BlockSpec((N_DIL, 3, hb, HEAD_DIM, DEC_BATCH), lambda b, n: (0, 0, b, 0, 0))]
    views = []
    for cache, (win, _) in zip(caches, DIL_PAIRS):
        views.append(jnp.transpose(cache, (0, 1, 3, 4, 5, 2)))
        in_specs.append(pl.BlockSpec((None, None, 2, hb, HEAD_DIM, win),
                                     lambda b, n: (layer, n, 0, b, 0, 0)))
    return pl.pallas_call(
        _attn_sample_kernel,
        grid=(N_HEADS // hb, DEC_BATCH),
        in_specs=in_specs,
        out_specs=pl.BlockSpec((hb, HEAD_DIM, DEC_BATCH), lambda b, n: (b, 0, 0)),
        out_shape=jax.ShapeDtypeStruct((N_HEADS, HEAD_DIM, DEC_BATCH), F32),
        compiler_params=_cparams(2),
        name="attn_sample",
    )(slopes, qkv_t, *views)


HIST = 16


def _pool_prompt_kernel(u_ref, wg_ref, sc_ref, z_ref, ext_ref, wgb_ref):
    i = pl.program_id(0)
    tiles_per_seq = SEQ // TILE

    @pl.when(i == 0)
    def _():
        wgb_ref[...] = wg_ref[...].astype(BF16)

    @pl.when(i % tiles_per_seq == 0)
    def _():
        ext_ref[0:HIST, :] = jnp.zeros((HIST, D_MODEL), F32)

    ext_ref[HIST:HIST + TILE, :] = u_ref[...]
    pos = (i % tiles_per_seq) * TILE + lax.broadcasted_iota(jnp.int32, (TILE, 1), 0)
    for g, win in enumerate(POOL_WINDOWS):
        cols = slice(g * POOL_CH, (g + 1) * POOL_CH)
        acc = ext_ref[HIST:HIST + TILE, cols]
        for k in range(1, win):
            acc = acc + ext_ref[HIST - k:HIST - k + TILE, cols]
        cnt = jnp.minimum(win, pos + 1).astype(F32)
        pooled = acc / cnt - u_ref[:, cols]
        z = jnp.dot(pooled.astype(BF16), wgb_ref[g], preferred_element_type=F32)
        z_ref[:, cols] = (z * sc_ref[:, cols]).astype(z_ref.dtype)
    ext_ref[0:HIST, :] = ext_ref[TILE:TILE + HIST, :]


def pool_prompt(u, w_group, scale):
    return pl.pallas_call(
        _pool_prompt_kernel,
        grid=(T_PROMPT // TILE,),
        in_specs=[pl.BlockSpec((TILE, D_MODEL), lambda i: (i, 0)),
                  pl.BlockSpec((None, len(POOL_WINDOWS), POOL_CH, POOL_CH), lambda i: (0, 0, 0, 0)),
                  pl.BlockSpec((1, D_MODEL), lambda i: (0, 0))],
        out_specs=pl.BlockSpec((TILE, D_MODEL), lambda i: (i, 0)),
        out_shape=jax.ShapeDtypeStruct((T_PROMPT, D_MODEL), BF16),
        scratch_shapes=[pltpu.VMEM((HIST + TILE, D_MODEL), F32),
                        pltpu.VMEM((len(POOL_WINDOWS), POOL_CH, POOL_CH), BF16)],
        compiler_params=_cparams(1),
        name="pool_prompt",
    )(u, w_group, scale)


def _pool_sample_kernel(u_ref, st_ref, wg_ref, sc_ref, z_ref):
    for g, win in enumerate(POOL_WINDOWS):
        cols = slice(g * POOL_CH, (g + 1) * POOL_CH)
        u = u_ref[:, cols]
        acc = u
        for k in range(1, win):
            acc = acc + st_ref[POOL_STATE - k, :, cols]
        cnt = float(min(win, PAST_LEN + 1))
        pooled = acc / cnt - u
        z = jnp.dot(pooled.astype(BF16), wg_ref[g].astype(BF16), preferred_element_type=F32)
        z_ref[:, cols] = (z * sc_ref[:, cols]).astype(z_ref.dtype)


def pool_sample(u, state_t, w_group, scale):
    return pl.pallas_call(
        _pool_sample_kernel,
        grid=(1,),
        in_specs=[pl.BlockSpec((DEC_BATCH, D_MODEL), lambda i: (0, 0)),
                  pl.BlockSpec((POOL_STATE, DEC_BATCH, D_MODEL), lambda i: (0, 0, 0)),
                  pl.BlockSpec((None, len(POOL_WINDOWS), POOL_CH, POOL_CH), lambda i: (0, 0, 0, 0)),
                  pl.BlockSpec((1, D_MODEL), lambda i: (0, 0))],
        out_specs=pl.BlockSpec((DEC_BATCH, D_MODEL), lambda i: (0, 0)),
        out_shape=jax.ShapeDtypeStruct((DEC_BATCH, D_MODEL), BF16),
        compiler_params=_cparams(1),
        name="pool_sample",
    )(u, state_t, w_group, scale)


CONV_HIST = 8


def _conv_prompt_kernel(p_ref, cw_ref, y_ref, ch_ref, ext_ref):
    i = pl.program_id(0)

    @pl.when(i % (SEQ // TILE) == 0)
    def _():
        ext_ref[0:CONV_HIST, :] = jnp.zeros((CONV_HIST, D_MODEL), F32)

    ch = p_ref[:, D_MODEL:2 * D_MODEL] * p_ref[:, 2 * D_MODEL:]
    ext_ref[CONV_HIST:CONV_HIST + TILE, :] = ch
    conv = cw_ref[0:1, :] * ext_ref[CONV_HIST - 2:CONV_HIST - 2 + TILE, :]
    conv = conv + cw_ref[1:2, :] * ext_ref[CONV_HIST - 1:CONV_HIST - 1 + TILE, :]
    conv = conv + cw_ref[2:3, :] * ch
    y_ref[...] = (p_ref[:, :D_MODEL] * conv).astype(y_ref.dtype)
    ch_ref[...] = ch
    ext_ref[0:CONV_HIST, :] = ext_ref[TILE:TILE + CONV_HIST, :]


def conv_prompt(p, conv_w):
    return pl.pallas_call(
        _conv_prompt_kernel,
        grid=(T_PROMPT // TILE,),
        in_specs=[pl.BlockSpec((TILE, 3 * D_MODEL), lambda i: (i, 0)),
                  pl.BlockSpec((None, 3, D_MODEL), lambda i: (0, 0, 0))],
        out_specs=[pl.BlockSpec((TILE, D_MODEL), lambda i: (i, 0)),
                   pl.BlockSpec((TILE, D_MODEL), lambda i: (i, 0))],
        out_shape=[jax.ShapeDtypeStruct((T_PROMPT, D_MODEL), BF16),
                   jax.ShapeDtypeStruct((T_PROMPT, D_MODEL), F32)],
        scratch_shapes=[pltpu.VMEM((CONV_HIST + TILE, D_MODEL), F32)],
        compiler_params=_cparams(1),
        name="conv_prompt",
    )(p, conv_w)


def _conv_sample_kernel(p_ref, st_ref, cw_ref, y_ref, ch_ref):
    ch = p_ref[:, D_MODEL:2 * D_MODEL] * p_ref[:, 2 * D_MODEL:]
    conv = cw_ref[0:1, :] * st_ref[0]
    conv = conv + cw_ref[1:2, :] * st_ref[1]
    conv = conv + cw_ref[2:3, :] * ch
    y_ref[...] = (p_ref[:, :D_MODEL] * conv).astype(y_ref.dtype)
    ch_ref[...] = ch


def conv_sample(p, state_t, conv_w):
    return pl.pallas_call(
        _conv_sample_kernel,
        grid=(1,),
        in_specs=[pl.BlockSpec((DEC_BATCH, 3 * D_MODEL), lambda i: (0, 0)),
                  pl.BlockSpec((2, DEC_BATCH, D_MODEL), lambda i: (0, 0, 0)),
                  pl.BlockSpec((None, 3, D_MODEL), lambda i: (0, 0, 0))],
        out_specs=[pl.BlockSpec((DEC_BATCH, D_MODEL), lambda i: (0, 0)),
                   pl.BlockSpec((DEC_BATCH, D_MODEL), lambda i: (0, 0))],
        out_shape=[jax.ShapeDtypeStruct((DEC_BATCH, D_MODEL), BF16),
                   jax.ShapeDtypeStruct((DEC_BATCH, D_MODEL), F32)],
        compiler_params=_cparams(1),
        name="conv_sample",
    )(p, state_t, conv_w)


def _tail_rows(sample_rows):
    return jnp.pad(sample_rows.astype(BF16), ((0, TAIL - DEC_BATCH), (0, 0)))


def kernel(x_prompt, x_sample, cache_kv_w128, cache_kv_w512, cache_kv_w2048, state_pool, state_conv, a_w_in, a_w_out, b_w_in, b_w_group, b_scale, b_w_out, c_w_in, c_conv, c_w_out, ln1_g, ln1_b, ln2_g, ln2_b, moe_w_group, moe_b_group, moe_w_expert, moe_b_expert, moe_w_up, moe_w_down):
    caches = (cache_kv_w128, cache_kv_w512, cache_kv_w2048)
    heads = jnp.arange(1, N_HEADS + 1, dtype=F32)
    slopes = jnp.exp2(-8.0 * heads / N_HEADS)

    x = jnp.concatenate([x_prompt.reshape(T_PROMPT, D_MODEL),
                         x_sample.reshape(DEC_BATCH, D_MODEL),
                         jnp.zeros((T_PAD - T_REAL, D_MODEL), F32)], axis=0)
    xb = x.astype(BF16)

    kv_p = [[] for _ in DIL_PAIRS]
    kv_s = [[] for _ in DIL_PAIRS]
    pool_p = pool_s = conv_p = conv_s = None

    for i in range(DEPTH):
        j, kind = i // N_MIXERS, i % N_MIXERS
        if kind == 0:
            proj, proj_t = project(xb, a_w_in, j, 1536, name=f"a_in_{i}")
            o = attn_prompt(proj, slopes)
            qkv_s = proj_t[:DEC_BATCH].reshape(DEC_BATCH, N_DIL, 3, N_HEADS, HEAD_DIM)
            o_t = attn_sample(jnp.transpose(qkv_s, (1, 2, 3, 4, 0)), caches, j, slopes)
            mixed, tail = o, _tail_rows(jnp.transpose(o_t, (2, 0, 1)).reshape(DEC_BATCH, D_MODEL))
            w_out, layer_w = a_w_out, j
            proj_p = proj.reshape(BATCH, SEQ, N_DIL, 3, N_HEADS, HEAD_DIM)
            for g, (win, _) in enumerate(DIL_PAIRS):
                kv_p[g].append(proj_p[:, SEQ - min(win, SEQ):, g, 1:3])
                kv_s[g].append(qkv_s[:, None, g, 1:3])
        elif kind == 1:
            u, u_t = project(xb, b_w_in, j, 1024, name=f"b_in_{i}")
            z = pool_prompt(u, b_w_group[j:j + 1], b_scale[j:j + 1])
            state_t = jnp.swapaxes(state_pool[j], 0, 1)
            z_s = pool_sample(u_t, state_t, b_w_group[j:j + 1], b_scale[j:j + 1])
            mixed, tail = z, _tail_rows(z_s)
            w_out, layer_w = b_w_out, j
            u_p = u.reshape(BATCH, SEQ, D_MODEL)
            u_s = u_t[:DEC_BATCH]
            pool_p = u_p[:, SEQ - POOL_STATE:][None]
            pool_s = jnp.concatenate([state_pool[j][:, 1:], u_s[:, None]], axis=1)[None]
        else:
            p3, p3_t = project(xb, c_w_in, j, 1536, name=f"c_in_{i}")
            y, ch = conv_prompt(p3, c_conv[j:j + 1])
            state_t = jnp.swapaxes(state_conv[j], 0, 1)
            y_s, ch_s = conv_sample(p3_t, state_t, c_conv[j:j + 1])
            mixed, tail = y, _tail_rows(y_s)
            w_out, layer_w = c_w_out, j
            conv_p = ch.reshape(BATCH, SEQ, D_MODEL)[:, SEQ - 2:][None]
            conv_s = jnp.concatenate([state_conv[j][:, 1:], ch_s[:, None]], axis=1)[None]

        wr = jnp.concatenate(
            [moe_w_group[i],
             jnp.transpose(moe_w_expert[i], (1, 0, 2)).reshape(D_MODEL, N_EXPERTS),
             jnp.zeros((D_MODEL, LANES - N_GROUPS - N_EXPERTS), F32)], axis=1)
        br = jnp.concatenate([moe_b_group[i], moe_b_expert[i].reshape(N_EXPERTS),
                              jnp.zeros((LANES - N_GROUPS - N_EXPERTS,), F32)])[None]
        h, hb, rw, ri = mix_out(mixed, tail, w_out, layer_w, x, ln1_g[i:i + 1], ln1_b[i:i + 1],
                                wr, br, name=f"mix_out_{i}")
        x, xb = sparse_moe(h, rw, ri, moe_w_up, moe_w_down, i, ln2_g[i:i + 1], ln2_b[i:i + 1], i)

    y_prompt = x[:T_PROMPT].reshape(BATCH, SEQ, D_MODEL)
    y_sample = x[T_PROMPT:T_REAL].reshape(DEC_BATCH, 1, D_MODEL)
    stack = lambda rows: jnp.stack(rows)
    return (y_prompt, y_sample,
            stack(kv_p[0]), stack(kv_s[0]), stack(kv_p[1]), stack(kv_s[1]),
            stack(kv_p[2]), stack(kv_s[2]), pool_p, pool_s, conv_p, conv_s)
```

```python
import functools

import jax
import jax.numpy as jnp
from jax import lax
from jax.experimental import pallas as pl
from jax.experimental.pallas import tpu as pltpu

F32 = jnp.float32
BF16 = jnp.bfloat16

D_MODEL = 1024
BATCH = 8
SEQ = 2048
DEPTH = 4
DEC_BATCH = 128
PAST_LEN = 2048
N_MIXERS = 3
N_HEADS = 16
HEAD_DIM = 64
DIL_PAIRS = ((128, 1), (512, 4), (2048, 16))
N_DIL = 3
BAND = 128
POOL_WINDOWS = (2, 4, 8, 16)
POOL_CH = 256
POOL_STATE = 15
N_GROUPS = 4
EXP_PER_GROUP = 8
N_EXPERTS = 32
EXPERT_FF = 256
DEEPNORM_ALPHA = (2 * DEPTH) ** 0.25
LN_EPS = 1e-5
NEG_INF = -1e30
SCALE = HEAD_DIM ** -0.5

T_PROMPT = BATCH * SEQ
T_REAL = T_PROMPT + DEC_BATCH
TILE = 512
T_PAD = 16896
TAIL = T_PAD - T_PROMPT
LANES = 128
VMEM_LIMIT = 56 * 1024 * 1024


def _cparams(n_axes, **kwargs):
    return pltpu.CompilerParams(dimension_semantics=("arbitrary",) * n_axes,
                                vmem_limit_bytes=VMEM_LIMIT, **kwargs)


def _mm_kernel(x_ref, w_ref, o_ref, wb_ref):
    @pl.when(pl.program_id(1) == 0)
    def _():
        wb_ref[...] = w_ref[...].astype(BF16)

    o_ref[...] = jnp.dot(x_ref[...], wb_ref[...],
                         preferred_element_type=F32).astype(o_ref.dtype)


def matmul(x, w, layer, row0, rows, tn, tm, name):
    k = x.shape[1]
    n = w.shape[-1]
    blk0 = row0 // tm
    return pl.pallas_call(
        _mm_kernel,
        grid=(n // tn, rows // tm),
        in_specs=[pl.BlockSpec((tm, k), lambda j, i: (blk0 + i, 0)),
                  pl.BlockSpec((None, k, tn), lambda j, i: (layer, 0, j))],
        out_specs=pl.BlockSpec((tm, tn), lambda j, i: (i, j)),
        out_shape=jax.ShapeDtypeStruct((rows, n), F32),
        scratch_shapes=[pltpu.VMEM((k, tn), BF16)],
        compiler_params=_cparams(2),
        name=name,
    )(x, w)


def project(xb, w, layer, tn_tail, name):
    prompt = matmul(xb, w, layer, 0, T_PROMPT, 1024, SEQ, name + "_p")
    tail = matmul(xb, w, layer, T_PROMPT, TAIL, tn_tail, TAIL, name + "_t")
    return prompt, tail


def _layer_norm(z, g, b):
    mu = jnp.mean(z, axis=-1, keepdims=True)
    zc = z - mu
    var = jnp.mean(zc * zc, axis=-1, keepdims=True)
    return zc * lax.rsqrt(var + LN_EPS) * g + b


def _route(logits):
    lane = lax.broadcasted_iota(jnp.int32, logits.shape, 1)
    gl = jnp.where(lane < N_GROUPS, logits, -jnp.inf)
    gmax = jnp.max(gl, axis=-1, keepdims=True)
    gsel = jnp.min(jnp.where(gl == gmax, lane, LANES), axis=-1, keepdims=True)
    p_g = 1.0 / jnp.sum(jnp.exp(gl - gmax), axis=-1, keepdims=True)
    lo = N_GROUPS + EXP_PER_GROUP * gsel
    el = jnp.where((lane >= lo) & (lane < lo + EXP_PER_GROUP), logits, -jnp.inf)
    v0 = jnp.max(el, axis=-1, keepdims=True)
    i0 = jnp.min(jnp.where(el == v0, lane, LANES), axis=-1, keepdims=True)
    el = jnp.where(lane == i0, -jnp.inf, el)
    v1 = jnp.max(el, axis=-1, keepdims=True)
    i1 = jnp.min(jnp.where(el == v1, lane, LANES), axis=-1, keepdims=True)
    e = jnp.exp(v1 - v0)
    q0 = 1.0 / (1.0 + e)
    q1 = e / (1.0 + e)
    rw = jnp.where(lane == 0, p_g * q0, jnp.where(lane == 1, p_g * q1, 0.0))
    ri = jnp.where(lane == 0, i0 - N_GROUPS, jnp.where(lane == 1, i1 - N_GROUPS, 0))
    return rw, ri


def _mix_out_kernel(a_ref, at_ref, w_ref, x_ref, g_ref, b_ref, wr_ref, br_ref,
                    h_ref, hb_ref, rw_ref, ri_ref, wb_ref, wrb_ref):
    i = pl.program_id(0)

    @pl.when(i == 0)
    def _():
        wb_ref[...] = w_ref[...].astype(BF16)
        wr = wr_ref[...]
        wr_hi = wr.astype(BF16)
        wrb_ref[:, :LANES] = wr_hi
        wrb_ref[:, LANES:] = (wr - wr_hi.astype(F32)).astype(BF16)

    a = jnp.where(i < T_PROMPT // TILE, a_ref[...], at_ref[...])
    y = jnp.dot(a, wb_ref[...], preferred_element_type=F32)
    h = _layer_norm(DEEPNORM_ALPHA * x_ref[...] + y, g_ref[...], b_ref[...])
    hb = h.astype(BF16)
    h_ref[...] = h
    hb_ref[...] = hb
    h_lo = (h - hb.astype(F32)).astype(BF16)
    split = jnp.dot(hb, wrb_ref[...], preferred_element_type=F32)
    logits = (split[:, :LANES] + split[:, LANES:]
              + jnp.dot(h_lo, wrb_ref[:, :LANES], preferred_element_type=F32) + br_ref[...])
    rw, ri = _route(logits)
    rw_ref[...] = rw
    ri_ref[...] = ri


def mix_out(a, a_tail, w_out, layer_w, x, g, b, wr, br, name):
    row = lambda i: (i, 0)
    fix = lambda i: (0, 0)
    return pl.pallas_call(
        _mix_out_kernel,
        grid=(T_PAD // TILE,),
        in_specs=[pl.BlockSpec((TILE, D_MODEL), lambda i: (jnp.minimum(i, T_PROMPT // TILE - 1), 0)),
                  pl.BlockSpec((TAIL, D_MODEL), fix),
                  pl.BlockSpec((None, D_MODEL, D_MODEL), lambda i: (layer_w, 0, 0)),
                  pl.BlockSpec((TILE, D_MODEL), row),
                  pl.BlockSpec((1, D_MODEL), fix),
                  pl.BlockSpec((1, D_MODEL), fix),
                  pl.BlockSpec((D_MODEL, LANES), fix),
                  pl.BlockSpec((1, LANES), fix)],
        out_specs=[pl.BlockSpec((TILE, D_MODEL), row),
                   pl.BlockSpec((TILE, D_MODEL), row),
                   pl.BlockSpec((TILE, LANES), row),
                   pl.BlockSpec((TILE, LANES), row)],
        out_shape=[jax.ShapeDtypeStruct((T_PAD, D_MODEL), F32),
                   jax.ShapeDtypeStruct((T_PAD, D_MODEL), BF16),
                   jax.ShapeDtypeStruct((T_PAD, LANES), F32),
                   jax.ShapeDtypeStruct((T_PAD, LANES), jnp.int32)],
        scratch_shapes=[pltpu.VMEM((D_MODEL, D_MODEL), BF16),
                        pltpu.VMEM((D_MODEL, 2 * LANES), BF16)],
        compiler_params=_cparams(1),
        name=name,
    )(a, a_tail, w_out, x, g, b, wr, br)


EXPERT_TILE = 512
N_ASSIGN = 2 * T_PAD
MAX_EXPERT_TILES = (N_ASSIGN + N_EXPERTS * (EXPERT_TILE - 1)) // EXPERT_TILE
SORTED_ROWS = MAX_EXPERT_TILES * EXPERT_TILE


def _plan_kernel(ri_ref, rank_ref, cnt_ref, base_ref):
    i = pl.program_id(0)

    @pl.when(i == 0)
    def _():
        base_ref[...] = jnp.zeros_like(base_ref)

    lane = lax.broadcasted_iota(jnp.int32, (TILE, LANES), 1)
    e0 = ri_ref[:, 0:1]
    e1 = ri_ref[:, 1:2]
    hit0 = lane == e0
    hit1 = lane == e1
    onehot = jnp.where(hit0 | hit1, 1.0, 0.0)
    r = lax.broadcasted_iota(jnp.int32, (TILE, TILE), 0)
    c = lax.broadcasted_iota(jnp.int32, (TILE, TILE), 1)
    earlier = jnp.where(c < r, 1.0, 0.0).astype(BF16)
    before = jnp.dot(earlier, onehot.astype(BF16), preferred_element_type=F32) + base_ref[...]
    rank0 = jnp.sum(jnp.where(hit0, before, 0.0), axis=-1, keepdims=True)
    rank1 = jnp.sum(jnp.where(hit1, before, 0.0), axis=-1, keepdims=True)
    rank_ref[...] = jnp.where(lane == 0, rank0, jnp.where(lane == 1, rank1, 0.0)).astype(jnp.int32)
    base_ref[...] = base_ref[...] + jnp.sum(onehot, axis=0, keepdims=True)
    cnt_ref[...] = jnp.broadcast_to(base_ref[...], cnt_ref.shape).astype(jnp.int32)


def moe_plan(ri, name):
    return pl.pallas_call(
        _plan_kernel,
        grid=(T_PAD // TILE,),
        in_specs=[pl.BlockSpec((TILE, LANES), lambda i: (i, 0))],
        out_specs=[pl.BlockSpec((TILE, LANES), lambda i: (i, 0)),
                   pl.BlockSpec((8, LANES), lambda i: (0, 0))],
        out_shape=[jax.ShapeDtypeStruct((T_PAD, LANES), jnp.int32),
                   jax.ShapeDtypeStruct((8, LANES), jnp.int32)],
        scratch_shapes=[pltpu.VMEM((1, LANES), F32)],
        compiler_params=_cparams(1),
        name=name,
    )(ri)


ROW_UNROLL = 8


def _for_each_row(body):
    def step(c, carry):
        for k in range(ROW_UNROLL):
            body(c * ROW_UNROLL + k)
        return carry

    lax.fori_loop(0, TILE // ROW_UNROLL, step, 0)


SUBLANES = 8
FILL_PIECES = tuple(2 ** k for k in range(EXPERT_TILE.bit_length() - 2, 2, -1))


def _zero_fill(fill_start_ref, fill_len_ref, n_tiles_ref, zeros_ref, xs_hbm, sem, wait):
    def run(copy):
        copy.wait() if wait else copy.start()

    for e in range(N_EXPERTS):
        start = fill_start_ref[e]
        head = (-start) & (SUBLANES - 1)
        for k in range(SUBLANES - 1):
            @pl.when(k < head)
            def _(k=k):
                run(pltpu.make_async_copy(zeros_ref.at[pl.ds(0, 1)], xs_hbm.at[pl.ds(start + k, 1)], sem))
        row = start + head
        length = fill_len_ref[e] - head
        for piece in FILL_PIECES:
            @pl.when((length & piece) != 0)
            def _(row=row, piece=piece):
                run(pltpu.make_async_copy(zeros_ref.at[pl.ds(0, piece)],
                                          xs_hbm.at[pl.ds(pl.multiple_of(row, SUBLANES), piece)], sem))
            row = row + (length & piece)

    def tail(t, carry):
        run(pltpu.make_async_copy(zeros_ref, xs_hbm.at[pl.ds(t * EXPERT_TILE, EXPERT_TILE)], sem))
        return carry

    lax.fori_loop(n_tiles_ref[0], MAX_EXPERT_TILES, tail, 0)


def _dispatch_kernel(pos_ref, fill_start_ref, fill_len_ref, n_tiles_ref, h_ref, xs_hbm,
                     zeros_ref, sem, fill_sem):
    i = pl.program_id(0)

    @pl.when(i == 0)
    def _():
        zeros_ref[...] = jnp.zeros_like(zeros_ref)
        _zero_fill(fill_start_ref, fill_len_ref, n_tiles_ref, zeros_ref, xs_hbm, fill_sem, wait=False)

    def row_copy(r, k):
        return pltpu.make_async_copy(h_ref.at[pl.ds(r, 1)], xs_hbm.at[pl.ds(pos_ref[0, 2 * r + k], 1)], sem)

    _for_each_row(lambda r: (row_copy(r, 0).start(), row_copy(r, 1).start()))
    _for_each_row(lambda r: (row_copy(r, 0).wait(), row_copy(r, 1).wait()))

    @pl.when(i == 0)
    def _():
        _zero_fill(fill_start_ref, fill_len_ref, n_tiles_ref, zeros_ref, xs_hbm, fill_sem, wait=True)


def moe_dispatch(pos, fill_start, fill_len, n_tiles, h, name):
    smem = pl.BlockSpec(memory_space=pltpu.SMEM)
    return pl.pallas_call(
        _dispatch_kernel,
        grid=(T_PAD // TILE,),
        in_specs=[pl.BlockSpec((None, 1, 2 * TILE), lambda i: (i, 0, 0), memory_space=pltpu.SMEM),
                  smem, smem, smem,
                  pl.BlockSpec((TILE, D_MODEL), lambda i: (i, 0))],
        out_specs=pl.BlockSpec(memory_space=pl.ANY),
        out_shape=jax.ShapeDtypeStruct((SORTED_ROWS, D_MODEL), F32),
        scratch_shapes=[pltpu.VMEM((EXPERT_TILE, D_MODEL), F32),
                        pltpu.SemaphoreType.DMA(()),
                        pltpu.SemaphoreType.DMA(())],
        compiler_params=pltpu.CompilerParams(dimension_semantics=("arbitrary",),
                                             has_side_effects=True, disable_bounds_checks=True),
        name=name,
    )(pos, fill_start, fill_len, n_tiles, h)


def _expert_kernel(tile_expert_ref, tile_first_ref, n_tiles_ref, xs_ref, wu_ref, wd_ref, ys_ref,
                   wub_ref, wdb_ref):
    t = pl.program_id(0)

    @pl.when(t < n_tiles_ref[0])
    def _():
        @pl.when(tile_first_ref[t] == 1)
        def _():
            wub_ref[...] = wu_ref[...].astype(BF16)
            wdb_ref[...] = wd_ref[...].astype(BF16)

        up = jnp.dot(xs_ref[...].astype(BF16), wub_ref[...], preferred_element_type=F32)
        a = up[:, :EXPERT_FF]
        u = up[:, EXPERT_FF:]
        hid = (a / (1.0 + jnp.exp(-a))) * u
        ys_ref[...] = jnp.dot(hid.astype(BF16), wdb_ref[...], preferred_element_type=F32)

    @pl.when(t >= n_tiles_ref[0])
    def _():
        ys_ref[...] = jnp.zeros_like(ys_ref)


def moe_experts(tile_expert, tile_first, n_tiles, xs, w_up, w_down, layer, name):
    def row_map(t, te, tf, nt):
        return (jnp.minimum(t, nt[0] - 1), 0)

    def out_map(t, te, tf, nt):
        return (t, 0)

    def w_map(t, te, tf, nt):
        return (layer, te[jnp.minimum(t, nt[0] - 1)], 0, 0)

    grid_spec = pltpu.PrefetchScalarGridSpec(
        num_scalar_prefetch=3,
        grid=(MAX_EXPERT_TILES,),
        in_specs=[pl.BlockSpec((EXPERT_TILE, D_MODEL), row_map),
                  pl.BlockSpec((None, None, D_MODEL, 2 * EXPERT_FF), w_map),
                  pl.BlockSpec((None, None, EXPERT_FF, D_MODEL), w_map)],
        out_specs=pl.BlockSpec((EXPERT_TILE, D_MODEL), out_map),
        scratch_shapes=[pltpu.VMEM((D_MODEL, 2 * EXPERT_FF), BF16),
                        pltpu.VMEM((EXPERT_FF, D_MODEL), BF16)])
    return pl.pallas_call(
        _expert_kernel,
        grid_spec=grid_spec,
        out_shape=jax.ShapeDtypeStruct((SORTED_ROWS, D_MODEL), F32),
        compiler_params=_cparams(1),
        name=name,
    )(tile_expert, tile_first, n_tiles, xs, w_up, w_down)


def _combine_kernel(pos_ref, h_ref, rw_ref, g_ref, b_ref, ys_hbm, x_ref, xb_ref, buf0, buf1, sem):
    bufs = (buf0, buf1)

    def row_copy(r, k):
        return pltpu.make_async_copy(ys_hbm.at[pl.ds(pos_ref[0, 2 * r + k], 1)], bufs[k].at[pl.ds(r, 1)], sem)

    _for_each_row(lambda r: (row_copy(r, 0).start(), row_copy(r, 1).start()))
    _for_each_row(lambda r: (row_copy(r, 0).wait(), row_copy(r, 1).wait()))
    f = rw_ref[:, 0:1] * buf0[...] + rw_ref[:, 1:2] * buf1[...]
    x = _layer_norm(DEEPNORM_ALPHA * h_ref[...] + f, g_ref[...], b_ref[...])
    x_ref[...] = x
    xb_ref[...] = x.astype(BF16)


def moe_combine(pos, h, rw, g, b, ys, name):
    row = lambda i: (i, 0)
    fix = lambda i: (0, 0)
    return pl.pallas_call(
        _combine_kernel,
        grid=(T_PAD // TILE,),
        in_specs=[pl.BlockSpec((None, 1, 2 * TILE), lambda i: (i, 0, 0), memory_space=pltpu.SMEM),
                  pl.BlockSpec((TILE, D_MODEL), row),
                  pl.BlockSpec((TILE, LANES), row),
                  pl.BlockSpec((1, D_MODEL), fix),
                  pl.BlockSpec((1, D_MODEL), fix),
                  pl.BlockSpec(memory_space=pl.ANY)],
        out_specs=[pl.BlockSpec((TILE, D_MODEL), row),
                   pl.BlockSpec((TILE, D_MODEL), row)],
        out_shape=[jax.ShapeDtypeStruct((T_PAD, D_MODEL), F32),
                   jax.ShapeDtypeStruct((T_PAD, D_MODEL), BF16)],
        scratch_shapes=[pltpu.VMEM((TILE, D_MODEL), F32),
                        pltpu.VMEM((TILE, D_MODEL), F32),
                        pltpu.SemaphoreType.DMA(())],
        compiler_params=_cparams(1, disable_bounds_checks=True),
        name=name,
    )(pos, h, rw, g, b, ys)


def sparse_moe(h, rw, ri, w_up, w_down, layer, g, b, tag):
    rank, cnt = moe_plan(ri, name=f"moe_plan_{tag}")
    counts = cnt[0, :N_EXPERTS]
    tiles = (counts + EXPERT_TILE - 1) // EXPERT_TILE
    tile_end = jnp.cumsum(tiles)
    tile_start = tile_end - tiles
    row_start = tile_start * EXPERT_TILE
    pos = jnp.take(row_start, ri[:, 0:2], axis=0) + rank[:, 0:2]
    pos = pos.reshape(T_PAD // TILE, 1, 2 * TILE)
    tile_ids = jnp.arange(MAX_EXPERT_TILES, dtype=jnp.int32)
    tile_expert = jnp.minimum(jnp.sum(tile_ids[:, None] >= tile_end[None, :], axis=1),
                              N_EXPERTS - 1).astype(jnp.int32)
    tile_first = jnp.any(tile_ids[:, None] == tile_start[None, :], axis=1).astype(jnp.int32)
    n_tiles = tile_end[-1:].astype(jnp.int32)
    fill_start = (row_start + counts).astype(jnp.int32)
    fill_len = (tiles * EXPERT_TILE - counts).astype(jnp.int32)
    xs = moe_dispatch(pos, fill_start, fill_len, n_tiles, h, name=f"moe_dispatch_{tag}")
    ys = moe_experts(tile_expert, tile_first, n_tiles, xs, w_up, w_down, layer, name=f"moe_experts_{tag}")
    return moe_combine(pos, h, rw, g, b, ys, name=f"moe_combine_{tag}")


def _softmax_pv(q, k, v, bias, valid):
    s = lax.dot_general(q, k, (((1,), (1,)), ((), ())), preferred_element_type=F32) * SCALE
    s = jnp.where(valid, s + bias, NEG_INF)
    m = jnp.max(s, axis=-1, keepdims=True)
    p = jnp.exp(s - m)
    den = jnp.sum(p, axis=-1, keepdims=True)
    pv = jnp.dot(p.astype(BF16), v, preferred_element_type=F32)
    return pv, m, den


def _attn_prompt_kernel(slopes_ref, q0, k0, v0, q1, k1, v1, q2, k2, v2, o_ref,
                        acc_o, acc_m, acc_d):
    hp = pl.program_id(1)
    q_refs, k_refs, v_refs = (q0, q1, q2), (k0, k1, k2), (v0, v1, v2)
    low = lax.broadcasted_iota(jnp.int32, (BAND, LANES), 1) < HEAD_DIM
    d2 = (lax.broadcasted_iota(jnp.int32, (BAND, 2 * BAND), 0) + BAND
          - lax.broadcasted_iota(jnp.int32, (BAND, 2 * BAND), 1))
    d1 = (lax.broadcasted_iota(jnp.int32, (BAND, BAND), 0)
          - lax.broadcasted_iota(jnp.int32, (BAND, BAND), 1))
    valid2 = (d2 >= 0) & (d2 <= BAND)
    valid1 = d1 >= 0
    d2f = d2.astype(F32)
    d1f = d1.astype(F32)
    slopes = (slopes_ref[2 * hp], slopes_ref[2 * hp + 1])

    for g, (_, dil) in enumerate(DIL_PAIRS):
        n_blk = SEQ // dil // BAND
        for r in range(dil):
            for b in range(n_blk):
                def rows(start, size):
                    return pl.ds(start, size, stride=dil) if dil > 1 else pl.ds(start, size)
                q_rows = rows(r + dil * BAND * b, BAND)
                if b == 0:
                    k_rows, distf, valid = rows(r, BAND), d1f, valid1
                else:
                    k_rows, distf, valid = rows(r + dil * BAND * (b - 1), 2 * BAND), d2f, valid2
                q = q_refs[g][q_rows, :]
                k = k_refs[g][k_rows, :].astype(BF16)
                v = v_refs[g][k_rows, :].astype(BF16)
                res = []
                for hh in range(2):
                    qm = jnp.where(low if hh == 0 else jnp.logical_not(low), q, 0.0).astype(BF16)
                    bias = distf * (-slopes[hh] * float(dil))
                    res.append(_softmax_pv(qm, k, v, bias, valid))
                acc_o[g, q_rows, :] = jnp.where(low, res[0][0], res[1][0])
                acc_m[g, q_rows, :] = jnp.where(low, res[0][1], res[1][1])
                acc_d[g, q_rows, :] = jnp.where(low, res[0][2], res[1][2])

    chunk = 256
    for c in range(SEQ // chunk):
        rws = pl.ds(c * chunk, chunk)
        m = [acc_m[g, rws, :] for g in range(N_DIL)]
        top = jnp.maximum(jnp.maximum(m[0], m[1]), m[2])
        num = jnp.zeros((chunk, LANES), F32)
        den = jnp.zeros((chunk, LANES), F32)
        for g in range(N_DIL):
            w = jnp.exp(m[g] - top)
            num = num + w * acc_o[g, rws, :]
            den = den + w * acc_d[g, rws, :]
        o_ref[rws, :] = (num / den).astype(o_ref.dtype)


def attn_prompt(proj, slopes):
    in_specs = [pl.BlockSpec(memory_space=pltpu.SMEM)]
    for g in range(N_DIL):
        for part in range(3):
            col0 = (g * 3 + part) * (D_MODEL // LANES)
            in_specs.append(pl.BlockSpec((SEQ, LANES),
                                         functools.partial(lambda n, hp, c: (n, c + hp), c=col0)))
    return pl.pallas_call(
        _attn_prompt_kernel,
        grid=(BATCH, N_HEADS // 2),
        in_specs=in_specs,
        out_specs=pl.BlockSpec((SEQ, LANES), lambda n, hp: (n, hp)),
        out_shape=jax.ShapeDtypeStruct((T_PROMPT, D_MODEL), BF16),
        scratch_shapes=[pltpu.VMEM((N_DIL, SEQ, LANES), F32)] * 3,
        compiler_params=_cparams(2),
        name="attn_prompt",
    )(slopes, *([proj] * 9))


N_A_LAYERS = 2


def _kv_transpose_kernel(a_ref, b_ref, o_ref):
    layer = pl.program_id(0)

    @pl.when(layer == 0)
    def _():
        o_ref[...] = a_ref[...].T

    @pl.when(layer == 1)
    def _():
        o_ref[...] = b_ref[...].T


def kv_transpose(proj_a, proj_b, g, win, name):
    rc = min(win, TILE)
    chunks = win // rc
    blocks_per_seq = SEQ // rc
    first = (SEQ - win) // rc

    def block(n, p, c):
        return (n * blocks_per_seq + first + c, g * 3 + 1 + p)

    a_last = block(BATCH - 1, 1, chunks - 1)
    b_first = block(0, 0, 0)

    def a_map(l, n, p, c):
        r, col = block(n, p, c)
        return (jnp.where(l == 0, r, a_last[0]), jnp.where(l == 0, col, a_last[1]))

    def b_map(l, n, p, c):
        r, col = block(n, p, c)
        return (jnp.where(l == 1, r, b_first[0]), jnp.where(l == 1, col, b_first[1]))

    return pl.pallas_call(
        _kv_transpose_kernel,
        grid=(N_A_LAYERS, BATCH, 2, chunks),
        in_specs=[pl.BlockSpec((rc, D_MODEL), a_map),
                  pl.BlockSpec((rc, D_MODEL), b_map)],
        out_specs=pl.BlockSpec((None, None, None, D_MODEL, rc), lambda l, n, p, c: (l, n, p, 0, c)),
        out_shape=jax.ShapeDtypeStruct((N_A_LAYERS, BATCH, 2, D_MODEL, win), F32),
        compiler_params=_cparams(4),
        name=name,
    )(proj_a, proj_b)


SAMPLE_HB = 8
SAMPLE_CHUNK = 512


def _attn_sample_kernel(slopes_ref, qkv_ref, c0_ref, c1_ref, c2_ref, o_ref):
    hb = pl.program_id(0)
    n = pl.program_id(1)
    caches = (c0_ref, c1_ref, c2_ref)

    @pl.when(n == 0)
    def _():
        o_ref[...] = jnp.zeros_like(o_ref)

    is_n = lax.broadcasted_iota(jnp.int32, (HEAD_DIM, DEC_BATCH), 1) == n

    def column(g, part, h):
        return jnp.sum(jnp.where(is_n, qkv_ref[g, part, h], 0.0), axis=1, keepdims=True)

    for h in range(SAMPLE_HB):
        neg_slope = -slopes_ref[hb * SAMPLE_HB + h]
        ms, nums, dens = [], [], []
        for g, (win, dil) in enumerate(DIL_PAIRS):
            q = column(g, 0, h)
            k_new = column(g, 1, h)
            v_new = column(g, 2, h)
            chunk = min(win, SAMPLE_CHUNK)
            s_new = jnp.sum(q * k_new, axis=0, keepdims=True) * SCALE
            scores = []
            m = s_new
            for c in range(win // chunk):
                k_t = caches[g][0, h, :, c * chunk:(c + 1) * chunk]
                s = jnp.sum(k_t * q, axis=0, keepdims=True) * SCALE
                pos = c * chunk + lax.broadcasted_iota(jnp.int32, (1, chunk), 1)
                s = s + (win - pos).astype(F32) * neg_slope
                if dil > 1:
                    s = jnp.where((pos & (dil - 1)) == 0, s, NEG_INF)
                scores.append(s)
                m = jnp.maximum(m, jnp.max(s, axis=1, keepdims=True))
            p_new = jnp.exp(s_new - m)
            den = p_new
            acc = jnp.zeros((HEAD_DIM, chunk), F32)
            for c, s in enumerate(scores):
                p = jnp.exp(s - m)
                den = den + jnp.sum(p, axis=1, keepdims=True)
                acc = acc + caches[g][1, h, :, c * chunk:(c + 1) * chunk] * p
            nums.append(jnp.sum(acc, axis=1, keepdims=True) + p_new * v_new)
            dens.append(den)
            ms.append(m)
        top = jnp.maximum(jnp.maximum(ms[0], ms[1]), ms[2])
        num = jnp.zeros((HEAD_DIM, 1), F32)
        den = jnp.zeros((1, 1), F32)
        for g in range(N_DIL):
            w = jnp.exp(ms[g] - top)
            num = num + w * nums[g]
            den = den + w * dens[g]
        o_ref[h] = jnp.where(is_n, num / den, o_ref[h])


def attn_sample(qkv_t, caches, layer, slopes):
    hb = SAMPLE_HB
    in_specs = [pl.BlockSpec(memory_space=pltpu.SMEM),
                pl.BlockSpec((N_DIL, 3, hb, HEAD_DIM, DEC_BATCH), lambda b, n: (0, 0, b, 0, 0))]
    views = []
    for cache, (win, _) in zip(caches, DIL_PAIRS):
        views.append(jnp.transpose(cache, (0, 1, 3, 4, 5, 2)))
        in_specs.append(pl.BlockSpec((None, None, 2, hb, HEAD_DIM, win),
                                     lambda b, n: (layer, n, 0, b, 0, 0)))
    return pl.pallas_call(
        _attn_sample_kernel,
        grid=(N_HEADS // hb, DEC_BATCH),
        in_specs=in_specs,
        out_specs=pl.BlockSpec((hb, HEAD_DIM, DEC_BATCH), lambda b, n: (b, 0, 0)),
        out_shape=jax.ShapeDtypeStruct((N_HEADS, HEAD_DIM, DEC_BATCH), F32),
        compiler_params=_cparams(2),
        name="attn_sample",
    )(slopes, qkv_t, *views)


HIST = 16


def _pool_prompt_kernel(u_ref, wg_ref, sc_ref, z_ref, ext_ref, wgb_ref):
    i = pl.program_id(0)
    tiles_per_seq = SEQ // TILE

    @pl.when(i == 0)
    def _():
        wgb_ref[...] = wg_ref[...].astype(BF16)

    @pl.when(i % tiles_per_seq == 0)
    def _():
        ext_ref[0:HIST, :] = jnp.zeros((HIST, D_MODEL), F32)

    ext_ref[HIST:HIST + TILE, :] = u_ref[...]
    pos = (i % tiles_per_seq) * TILE + lax.broadcasted_iota(jnp.int32, (TILE, 1), 0)
    for g, win in enumerate(POOL_WINDOWS):
        cols = slice(g * POOL_CH, (g + 1) * POOL_CH)
        acc = ext_ref[HIST:HIST + TILE, cols]
        for k in range(1, win):
            acc = acc + ext_ref[HIST - k:HIST - k + TILE, cols]
        cnt = jnp.minimum(win, pos + 1).astype(F32)
        pooled = acc / cnt - u_ref[:, cols]
        z = jnp.dot(pooled.astype(BF16), wgb_ref[g], preferred_element_type=F32)
        z_ref[:, cols] = (z * sc_ref[:, cols]).astype(z_ref.dtype)
    ext_ref[0:HIST, :] = ext_ref[TILE:TILE + HIST, :]


def pool_prompt(u, w_group, scale):
    return pl.pallas_call(
        _pool_prompt_kernel,
        grid=(T_PROMPT // TILE,),
        in_specs=[pl.BlockSpec((TILE, D_MODEL), lambda i: (i, 0)),
                  pl.BlockSpec((None, len(POOL_WINDOWS), POOL_CH, POOL_CH), lambda i: (0, 0, 0, 0)),
                  pl.BlockSpec((1, D_MODEL), lambda i: (0, 0))],
        out_specs=pl.BlockSpec((TILE, D_MODEL), lambda i: (i, 0)),
        out_shape=jax.ShapeDtypeStruct((T_PROMPT, D_MODEL), BF16),
        scratch_shapes=[pltpu.VMEM((HIST + TILE, D_MODEL), F32),
                        pltpu.VMEM((len(POOL_WINDOWS), POOL_CH, POOL_CH), BF16)],
        compiler_params=_cparams(1),
        name="pool_prompt",
    )(u, w_group, scale)


def _pool_sample_kernel(u_ref, st_ref, wg_ref, sc_ref, z_ref):
    for g, win in enumerate(POOL_WINDOWS):
        cols = slice(g * POOL_CH, (g + 1) * POOL_CH)
        u = u_ref[:, cols]
        acc = u
        for k in range(1, win):
            acc = acc + st_ref[POOL_STATE - k, :, cols]
        cnt = float(min(win, PAST_LEN + 1))
        pooled = acc / cnt - u
        z = jnp.dot(pooled.astype(BF16), wg_ref[g].astype(BF16), preferred_element_type=F32)
        z_ref[:, cols] = (z * sc_ref[:, cols]).astype(z_ref.dtype)


def pool_sample(u, state_t, w_group, scale):
    return pl.pallas_call(
        _pool_sample_kernel,
        grid=(1,),
        in_specs=[pl.BlockSpec((DEC_BATCH, D_MODEL), lambda i: (0, 0)),
                  pl.BlockSpec((POOL_STATE, DEC_BATCH, D_MODEL), lambda i: (0, 0, 0)),
                  pl.BlockSpec((None, len(POOL_WINDOWS), POOL_CH, POOL_CH), lambda i: (0, 0, 0, 0)),
                  pl.BlockSpec((1, D_MODEL), lambda i: (0, 0))],
        out_specs=pl.BlockSpec((DEC_BATCH, D_MODEL), lambda i: (0, 0)),
        out_shape=jax.ShapeDtypeStruct((DEC_BATCH, D_MODEL), BF16),
        compiler_params=_cparams(1),
        name="pool_sample",
    )(u, state_t, w_group, scale)


CONV_HIST = 8


def _conv_prompt_kernel(p_ref, cw_ref, y_ref, ch_ref, ext_ref):
    i = pl.program_id(0)

    @pl.when(i % (SEQ // TILE) == 0)
    def _():
        ext_ref[0:CONV_HIST, :] = jnp.zeros((CONV_HIST, D_MODEL), F32)

    ch = p_ref[:, D_MODEL:2 * D_MODEL] * p_ref[:, 2 * D_MODEL:]
    ext_ref[CONV_HIST:CONV_HIST + TILE, :] = ch
    conv = cw_ref[0:1, :] * ext_ref[CONV_HIST - 2:CONV_HIST - 2 + TILE, :]
    conv = conv + cw_ref[1:2, :] * ext_ref[CONV_HIST - 1:CONV_HIST - 1 + TILE, :]
    conv = conv + cw_ref[2:3, :] * ch
    y_ref[...] = (p_ref[:, :D_MODEL] * conv).astype(y_ref.dtype)
    ch_ref[...] = ch
    ext_ref[0:CONV_HIST, :] = ext_ref[TILE:TILE + CONV_HIST, :]


def conv_prompt(p, conv_w):
    return pl.pallas_call(
        _conv_prompt_kernel,
        grid=(T_PROMPT // TILE,),
        in_specs=[pl.BlockSpec((TILE, 3 * D_MODEL), lambda i: (i, 0)),
                  pl.BlockSpec((None, 3, D_MODEL), lambda i: (0, 0, 0))],
        out_specs=[pl.BlockSpec((TILE, D_MODEL), lambda i: (i, 0)),
                   pl.BlockSpec((TILE, D_MODEL), lambda i: (i, 0))],
        out_shape=[jax.ShapeDtypeStruct((T_PROMPT, D_MODEL), BF16),
                   jax.ShapeDtypeStruct((T_PROMPT, D_MODEL), F32)],
        scratch_shapes=[pltpu.VMEM((CONV_HIST + TILE, D_MODEL), F32)],
        compiler_params=_cparams(1),
        name="conv_prompt",
    )(p, conv_w)


def _conv_sample_kernel(p_ref, st_ref, cw_ref, y_ref, ch_ref):
    ch = p_ref[:, D_MODEL:2 * D_MODEL] * p_ref[:, 2 * D_MODEL:]
    conv = cw_ref[0:1, :] * st_ref[0]
    conv = conv + cw_ref[1:2, :] * st_ref[1]
    conv = conv + cw_ref[2:3, :] * ch
    y_ref[...] = (p_ref[:, :D_MODEL] * conv).astype(y_ref.dtype)
    ch_ref[...] = ch


def conv_sample(p, state_t, conv_w):
    return pl.pallas_call(
        _conv_sample_kernel,
        grid=(1,),
        in_specs=[pl.BlockSpec((DEC_BATCH, 3 * D_MODEL), lambda i: (0, 0)),
                  pl.BlockSpec((2, DEC_BATCH, D_MODEL), lambda i: (0, 0, 0)),
                  pl.BlockSpec((None, 3, D_MODEL), lambda i: (0, 0, 0))],
        out_specs=[pl.BlockSpec((DEC_BATCH, D_MODEL), lambda i: (0, 0)),
                   pl.BlockSpec((DEC_BATCH, D_MODEL), lambda i: (0, 0))],
        out_shape=[jax.ShapeDtypeStruct((DEC_BATCH, D_MODEL), BF16),
                   jax.ShapeDtypeStruct((DEC_BATCH, D_MODEL), F32)],
        compiler_params=_cparams(1),
        name="conv_sample",
    )(p, state_t, conv_w)


def _tail_rows(sample_rows):
    return jnp.pad(sample_rows.astype(BF16), ((0, TAIL - DEC_BATCH), (0, 0)))


def kernel(x_prompt, x_sample, cache_kv_w128, cache_kv_w512, cache_kv_w2048, state_pool, state_conv, a_w_in, a_w_out, b_w_in, b_w_group, b_scale, b_w_out, c_w_in, c_conv, c_w_out, ln1_g, ln1_b, ln2_g, ln2_b, moe_w_group, moe_b_group, moe_w_expert, moe_b_expert, moe_w_up, moe_w_down):
    caches = (cache_kv_w128, cache_kv_w512, cache_kv_w2048)
    heads = jnp.arange(1, N_HEADS + 1, dtype=F32)
    slopes = jnp.exp2(-8.0 * heads / N_HEADS)

    x = jnp.concatenate([x_prompt.reshape(T_PROMPT, D_MODEL),
                         x_sample.reshape(DEC_BATCH, D_MODEL),
                         jnp.zeros((T_PAD - T_REAL, D_MODEL), F32)], axis=0)
    xb = x.astype(BF16)

    projs = []
    kv_s = [[] for _ in DIL_PAIRS]
    pool_p = pool_s = conv_p = conv_s = None

    for i in range(DEPTH):
        j, kind = i // N_MIXERS, i % N_MIXERS
        if kind == 0:
            proj, proj_t = project(xb, a_w_in, j, 1536, name=f"a_in_{i}")
            o = attn_prompt(proj, slopes)
            qkv_s = proj_t[:DEC_BATCH].reshape(DEC_BATCH, N_DIL, 3, N_HEADS, HEAD_DIM)
            o_t = attn_sample(jnp.transpose(qkv_s, (1, 2, 3, 4, 0)), caches, j, slopes)
            mixed, tail = o, _tail_rows(jnp.transpose(o_t, (2, 0, 1)).reshape(DEC_BATCH, D_MODEL))
            w_out, layer_w = a_w_out, j
            projs.append(proj)
            for g in range(N_DIL):
                kv_s[g].append(qkv_s[:, None, g, 1:3])
        elif kind == 1:
            u, u_t = project(xb, b_w_in, j, 1024, name=f"b_in_{i}")
            z = pool_prompt(u, b_w_group[j:j + 1], b_scale[j:j + 1])
            state_t = jnp.swapaxes(state_pool[j], 0, 1)
            z_s = pool_sample(u_t, state_t, b_w_group[j:j + 1], b_scale[j:j + 1])
            mixed, tail = z, _tail_rows(z_s)
            w_out, layer_w = b_w_out, j
            u_p = u.reshape(BATCH, SEQ, D_MODEL)
            u_s = u_t[:DEC_BATCH]
            pool_p = u_p[:, SEQ - POOL_STATE:][None]
            pool_s = jnp.concatenate([state_pool[j][:, 1:], u_s[:, None]], axis=1)[None]
        else:
            p3, p3_t = project(xb, c_w_in, j, 1536, name=f"c_in_{i}")
            y, ch = conv_prompt(p3, c_conv[j:j + 1])
            state_t = jnp.swapaxes(state_conv[j], 0, 1)
            y_s, ch_s = conv_sample(p3_t, state_t, c_conv[j:j + 1])
            mixed, tail = y, _tail_rows(y_s)
            w_out, layer_w = c_w_out, j
            conv_p = ch.reshape(BATCH, SEQ, D_MODEL)[:, SEQ - 2:][None]
            conv_s = jnp.concatenate([state_conv[j][:, 1:], ch_s[:, None]], axis=1)[None]

        wr = jnp.concatenate(
            [moe_w_group[i],
             jnp.transpose(moe_w_expert[i], (1, 0, 2)).reshape(D_MODEL, N_EXPERTS),
             jnp.zeros((D_MODEL, LANES - N_GROUPS - N_EXPERTS), F32)], axis=1)
        br = jnp.concatenate([moe_b_group[i], moe_b_expert[i].reshape(N_EXPERTS),
                              jnp.zeros((LANES - N_GROUPS - N_EXPERTS,), F32)])[None]
        h, hb, rw, ri = mix_out(mixed, tail, w_out, layer_w, x, ln1_g[i:i + 1], ln1_b[i:i + 1],
                                wr, br, name=f"mix_out_{i}")
        x, xb = sparse_moe(h, rw, ri, moe_w_up, moe_w_down, i, ln2_g[i:i + 1], ln2_b[i:i + 1], i)

    y_prompt = x[:T_PROMPT].reshape(BATCH, SEQ, D_MODEL)
    y_sample = x[T_PROMPT:T_REAL].reshape(DEC_BATCH, 1, D_MODEL)
    kv_p = []
    for g, (win, _) in enumerate(DIL_PAIRS):
        kv_t = kv_transpose(projs[0], projs[1], g, win, name=f"kv_out_{g}")
        kv_t = kv_t.reshape(N_A_LAYERS, BATCH, 2, N_HEADS, HEAD_DIM, win)
        kv_p.append(jnp.transpose(kv_t, (0, 1, 5, 2, 3, 4)))
    stack = lambda rows: jnp.stack(rows)
    return (y_prompt, y_sample,
            kv_p[0], stack(kv_s[0]), kv_p[1], stack(kv_s[1]),
            kv_p[2], stack(kv_s[2]), pool_p, pool_s, conv_p, conv_s)
```

```python
import functools

import jax
import jax.numpy as jnp
from jax import lax
from jax.experimental import pallas as pl
from jax.experimental.pallas import tpu as pltpu

F32 = jnp.float32
BF16 = jnp.bfloat16

D_MODEL = 1024
BATCH = 8
SEQ = 2048
DEPTH = 4
DEC_BATCH = 128
PAST_LEN = 2048
N_MIXERS = 3
N_HEADS = 16
HEAD_DIM = 64
DIL_PAIRS = ((128, 1), (512, 4), (2048, 16))
N_DIL = 3
BAND = 128
POOL_WINDOWS = (2, 4, 8, 16)
POOL_CH = 256
POOL_STATE = 15
N_GROUPS = 4
EXP_PER_GROUP = 8
N_EXPERTS = 32
EXPERT_FF = 256
DEEPNORM_ALPHA = (2 * DEPTH) ** 0.25
LN_EPS = 1e-5
NEG_INF = -1e30
SCALE = HEAD_DIM ** -0.5

T_PROMPT = BATCH * SEQ
T_REAL = T_PROMPT + DEC_BATCH
TILE = 512
T_PAD = 16896
TAIL = T_PAD - T_PROMPT
LANES = 128
VMEM_LIMIT = 56 * 1024 * 1024


def _cparams(n_axes, **kwargs):
    return pltpu.CompilerParams(dimension_semantics=("arbitrary",) * n_axes,
                                vmem_limit_bytes=VMEM_LIMIT, **kwargs)


def _mm_kernel(x_ref, w_ref, o_ref, wb_ref):
    @pl.when(pl.program_id(1) == 0)
    def _():
        wb_ref[...] = w_ref[...].astype(BF16)

    o_ref[...] = jnp.dot(x_ref[...], wb_ref[...],
                         preferred_element_type=F32).astype(o_ref.dtype)


def matmul(x, w, layer, row0, rows, tn, tm, name):
    k = x.shape[1]
    n = w.shape[-1]
    blk0 = row0 // tm
    return pl.pallas_call(
        _mm_kernel,
        grid=(n // tn, rows // tm),
        in_specs=[pl.BlockSpec((tm, k), lambda j, i: (blk0 + i, 0)),
                  pl.BlockSpec((None, k, tn), lambda j, i: (layer, 0, j))],
        out_specs=pl.BlockSpec((tm, tn), lambda j, i: (i, j)),
        out_shape=jax.ShapeDtypeStruct((rows, n), F32),
        scratch_shapes=[pltpu.VMEM((k, tn), BF16)],
        compiler_params=_cparams(2),
        name=name,
    )(x, w)


def project(xb, w, layer, tn_tail, name):
    prompt = matmul(xb, w, layer, 0, T_PROMPT, 1024, SEQ, name + "_p")
    tail = matmul(xb, w, layer, T_PROMPT, TAIL, tn_tail, TAIL, name + "_t")
    return prompt, tail


def _layer_norm(z, g, b):
    mu = jnp.mean(z, axis=-1, keepdims=True)
    zc = z - mu
    var = jnp.mean(zc * zc, axis=-1, keepdims=True)
    return zc * lax.rsqrt(var + LN_EPS) * g + b


def _route(logits):
    lane = lax.broadcasted_iota(jnp.int32, logits.shape, 1)
    gl = jnp.where(lane < N_GROUPS, logits, -jnp.inf)
    gmax = jnp.max(gl, axis=-1, keepdims=True)
    gsel = jnp.min(jnp.where(gl == gmax, lane, LANES), axis=-1, keepdims=True)
    p_g = 1.0 / jnp.sum(jnp.exp(gl - gmax), axis=-1, keepdims=True)
    lo = N_GROUPS + EXP_PER_GROUP * gsel
    el = jnp.where((lane >= lo) & (lane < lo + EXP_PER_GROUP), logits, -jnp.inf)
    v0 = jnp.max(el, axis=-1, keepdims=True)
    i0 = jnp.min(jnp.where(el == v0, lane, LANES), axis=-1, keepdims=True)
    el = jnp.where(lane == i0, -jnp.inf, el)
    v1 = jnp.max(el, axis=-1, keepdims=True)
    i1 = jnp.min(jnp.where(el == v1, lane, LANES), axis=-1, keepdims=True)
    e = jnp.exp(v1 - v0)
    q0 = 1.0 / (1.0 + e)
    q1 = e / (1.0 + e)
    rw = jnp.where(lane == 0, p_g * q0, jnp.where(lane == 1, p_g * q1, 0.0))
    ri = jnp.where(lane == 0, i0 - N_GROUPS, jnp.where(lane == 1, i1 - N_GROUPS, 0))
    return rw, ri


def _mix_out_kernel(a_ref, at_ref, w_ref, x_ref, g_ref, b_ref, wr_ref, br_ref,
                    h_ref, hb_ref, rw_ref, ri_ref, wb_ref, wrb_ref):
    i = pl.program_id(0)

    @pl.when(i == 0)
    def _():
        wb_ref[...] = w_ref[...].astype(BF16)
        wr = wr_ref[...]
        wr_hi = wr.astype(BF16)
        wrb_ref[:, :LANES] = wr_hi
        wrb_ref[:, LANES:] = (wr - wr_hi.astype(F32)).astype(BF16)

    a = jnp.where(i < T_PROMPT // TILE, a_ref[...], at_ref[...])
    y = jnp.dot(a, wb_ref[...], preferred_element_type=F32)
    h = _layer_norm(DEEPNORM_ALPHA * x_ref[...] + y, g_ref[...], b_ref[...])
    hb = h.astype(BF16)
    h_ref[...] = h
    hb_ref[...] = hb
    h_lo = (h - hb.astype(F32)).astype(BF16)
    split = jnp.dot(hb, wrb_ref[...], preferred_element_type=F32)
    logits = (split[:, :LANES] + split[:, LANES:]
              + jnp.dot(h_lo, wrb_ref[:, :LANES], preferred_element_type=F32) + br_ref[...])
    rw, ri = _route(logits)
    rw_ref[...] = rw
    ri_ref[...] = ri


def mix_out(a, a_tail, w_out, layer_w, x, g, b, wr, br, name):
    row = lambda i: (i, 0)
    fix = lambda i: (0, 0)
    return pl.pallas_call(
        _mix_out_kernel,
        grid=(T_PAD // TILE,),
        in_specs=[pl.BlockSpec((TILE, D_MODEL), lambda i: (jnp.minimum(i, T_PROMPT // TILE - 1), 0)),
                  pl.BlockSpec((TAIL, D_MODEL), fix),
                  pl.BlockSpec((None, D_MODEL, D_MODEL), lambda i: (layer_w, 0, 0)),
                  pl.BlockSpec((TILE, D_MODEL), row),
                  pl.BlockSpec((1, D_MODEL), fix),
                  pl.BlockSpec((1, D_MODEL), fix),
                  pl.BlockSpec((D_MODEL, LANES), fix),
                  pl.BlockSpec((1, LANES), fix)],
        out_specs=[pl.BlockSpec((TILE, D_MODEL), row),
                   pl.BlockSpec((TILE, D_MODEL), row),
                   pl.BlockSpec((TILE, LANES), row),
                   pl.BlockSpec((TILE, LANES), row)],
        out_shape=[jax.ShapeDtypeStruct((T_PAD, D_MODEL), F32),
                   jax.ShapeDtypeStruct((T_PAD, D_MODEL), BF16),
                   jax.ShapeDtypeStruct((T_PAD, LANES), F32),
                   jax.ShapeDtypeStruct((T_PAD, LANES), jnp.int32)],
        scratch_shapes=[pltpu.VMEM((D_MODEL, D_MODEL), BF16),
                        pltpu.VMEM((D_MODEL, 2 * LANES), BF16)],
        compiler_params=_cparams(1),
        name=name,
    )(a, a_tail, w_out, x, g, b, wr, br)


EXPERT_TILE = 512
N_ASSIGN = 2 * T_PAD
MAX_EXPERT_TILES = (N_ASSIGN + N_EXPERTS * (EXPERT_TILE - 1)) // EXPERT_TILE
SORTED_ROWS = MAX_EXPERT_TILES * EXPERT_TILE


def _plan_kernel(ri_ref, rank_ref, cnt_ref, base_ref):
    i = pl.program_id(0)

    @pl.when(i == 0)
    def _():
        base_ref[...] = jnp.zeros_like(base_ref)

    lane = lax.broadcasted_iota(jnp.int32, (TILE, LANES), 1)
    e0 = ri_ref[:, 0:1]
    e1 = ri_ref[:, 1:2]
    hit0 = lane == e0
    hit1 = lane == e1
    onehot = jnp.where(hit0 | hit1, 1.0, 0.0)
    r = lax.broadcasted_iota(jnp.int32, (TILE, TILE), 0)
    c = lax.broadcasted_iota(jnp.int32, (TILE, TILE), 1)
    earlier = jnp.where(c < r, 1.0, 0.0).astype(BF16)
    before = jnp.dot(earlier, onehot.astype(BF16), preferred_element_type=F32) + base_ref[...]
    rank0 = jnp.sum(jnp.where(hit0, before, 0.0), axis=-1, keepdims=True)
    rank1 = jnp.sum(jnp.where(hit1, before, 0.0), axis=-1, keepdims=True)
    rank_ref[...] = jnp.where(lane == 0, rank0, jnp.where(lane == 1, rank1, 0.0)).astype(jnp.int32)
    base_ref[...] = base_ref[...] + jnp.sum(onehot, axis=0, keepdims=True)
    cnt_ref[...] = jnp.broadcast_to(base_ref[...], cnt_ref.shape).astype(jnp.int32)


def moe_plan(ri, name):
    return pl.pallas_call(
        _plan_kernel,
        grid=(T_PAD // TILE,),
        in_specs=[pl.BlockSpec((TILE, LANES), lambda i: (i, 0))],
        out_specs=[pl.BlockSpec((TILE, LANES), lambda i: (i, 0)),
                   pl.BlockSpec((8, LANES), lambda i: (0, 0))],
        out_shape=[jax.ShapeDtypeStruct((T_PAD, LANES), jnp.int32),
                   jax.ShapeDtypeStruct((8, LANES), jnp.int32)],
        scratch_shapes=[pltpu.VMEM((1, LANES), F32)],
        compiler_params=_cparams(1),
        name=name,
    )(ri)


ROW_UNROLL = 8


def _for_each_row(body):
    def step(c, carry):
        for k in range(ROW_UNROLL):
            body(c * ROW_UNROLL + k)
        return carry

    lax.fori_loop(0, TILE // ROW_UNROLL, step, 0)


SUBLANES = 8
FILL_PIECES = tuple(2 ** k for k in range(EXPERT_TILE.bit_length() - 2, 2, -1))


def _zero_fill(fill_start_ref, fill_len_ref, n_tiles_ref, zeros_ref, xs_hbm, sem, wait):
    def run(copy):
        copy.wait() if wait else copy.start()

    for e in range(N_EXPERTS):
        start = fill_start_ref[e]
        head = (-start) & (SUBLANES - 1)
        for k in range(SUBLANES - 1):
            @pl.when(k < head)
            def _(k=k):
                run(pltpu.make_async_copy(zeros_ref.at[pl.ds(0, 1)], xs_hbm.at[pl.ds(start + k, 1)], sem))
        row = start + head
        length = fill_len_ref[e] - head
        for piece in FILL_PIECES:
            @pl.when((length & piece) != 0)
            def _(row=row, piece=piece):
                run(pltpu.make_async_copy(zeros_ref.at[pl.ds(0, piece)],
                                          xs_hbm.at[pl.ds(pl.multiple_of(row, SUBLANES), piece)], sem))
            row = row + (length & piece)

    def tail(t, carry):
        run(pltpu.make_async_copy(zeros_ref, xs_hbm.at[pl.ds(t * EXPERT_TILE, EXPERT_TILE)], sem))
        return carry

    lax.fori_loop(n_tiles_ref[0], MAX_EXPERT_TILES, tail, 0)


def _dispatch_kernel(pos_ref, fill_start_ref, fill_len_ref, n_tiles_ref, h_ref, xs_hbm,
                     zeros_ref, sem, fill_sem):
    i = pl.program_id(0)

    @pl.when(i == 0)
    def _():
        zeros_ref[...] = jnp.zeros_like(zeros_ref)
        _zero_fill(fill_start_ref, fill_len_ref, n_tiles_ref, zeros_ref, xs_hbm, fill_sem, wait=False)

    def row_copy(r, k):
        return pltpu.make_async_copy(h_ref.at[pl.ds(r, 1)], xs_hbm.at[pl.ds(pos_ref[0, 2 * r + k], 1)], sem)

    _for_each_row(lambda r: (row_copy(r, 0).start(), row_copy(r, 1).start()))
    _for_each_row(lambda r: (row_copy(r, 0).wait(), row_copy(r, 1).wait()))

    @pl.when(i == 0)
    def _():
        _zero_fill(fill_start_ref, fill_len_ref, n_tiles_ref, zeros_ref, xs_hbm, fill_sem, wait=True)


def moe_dispatch(pos, fill_start, fill_len, n_tiles, h, name):
    smem = pl.BlockSpec(memory_space=pltpu.SMEM)
    return pl.pallas_call(
        _dispatch_kernel,
        grid=(T_PAD // TILE,),
        in_specs=[pl.BlockSpec((None, 1, 2 * TILE), lambda i: (i, 0, 0), memory_space=pltpu.SMEM),
                  smem, smem, smem,
                  pl.BlockSpec((TILE, D_MODEL), lambda i: (i, 0))],
        out_specs=pl.BlockSpec(memory_space=pl.ANY),
        out_shape=jax.ShapeDtypeStruct((SORTED_ROWS, D_MODEL), F32),
        scratch_shapes=[pltpu.VMEM((EXPERT_TILE, D_MODEL), F32),
                        pltpu.SemaphoreType.DMA(()),
                        pltpu.SemaphoreType.DMA(())],
        compiler_params=pltpu.CompilerParams(dimension_semantics=("arbitrary",),
                                             has_side_effects=True, disable_bounds_checks=True),
        name=name,
    )(pos, fill_start, fill_len, n_tiles, h)


def _expert_kernel(tile_expert_ref, tile_first_ref, n_tiles_ref, xs_ref, wu_ref, wd_ref, ys_ref,
                   wub_ref, wdb_ref):
    t = pl.program_id(0)

    @pl.when(t < n_tiles_ref[0])
    def _():
        @pl.when(tile_first_ref[t] == 1)
        def _():
            wub_ref[...] = wu_ref[...].astype(BF16)
            wdb_ref[...] = wd_ref[...].astype(BF16)

        up = jnp.dot(xs_ref[...].astype(BF16), wub_ref[...], preferred_element_type=F32)
        a = up[:, :EXPERT_FF]
        u = up[:, EXPERT_FF:]
        hid = (a / (1.0 + jnp.exp(-a))) * u
        ys_ref[...] = jnp.dot(hid.astype(BF16), wdb_ref[...], preferred_element_type=F32)

    @pl.when(t >= n_tiles_ref[0])
    def _():
        ys_ref[...] = jnp.zeros_like(ys_ref)


def moe_experts(tile_expert, tile_first, n_tiles, xs, w_up, w_down, layer, name):
    def row_map(t, te, tf, nt):
        return (jnp.minimum(t, nt[0] - 1), 0)

    def out_map(t, te, tf, nt):
        return (t, 0)

    def w_map(t, te, tf, nt):
        return (layer, te[jnp.minimum(t, nt[0] - 1)], 0, 0)

    grid_spec = pltpu.PrefetchScalarGridSpec(
        num_scalar_prefetch=3,
        grid=(MAX_EXPERT_TILES,),
        in_specs=[pl.BlockSpec((EXPERT_TILE, D_MODEL), row_map),
                  pl.BlockSpec((None, None, D_MODEL, 2 * EXPERT_FF), w_map),
                  pl.BlockSpec((None, None, EXPERT_FF, D_MODEL), w_map)],
        out_specs=pl.BlockSpec((EXPERT_TILE, D_MODEL), out_map),
        scratch_shapes=[pltpu.VMEM((D_MODEL, 2 * EXPERT_FF), BF16),
                        pltpu.VMEM((EXPERT_FF, D_MODEL), BF16)])
    return pl.pallas_call(
        _expert_kernel,
        grid_spec=grid_spec,
        out_shape=jax.ShapeDtypeStruct((SORTED_ROWS, D_MODEL), F32),
        compiler_params=_cparams(1),
        name=name,
    )(tile_expert, tile_first, n_tiles, xs, w_up, w_down)


def _combine_kernel(pos_ref, h_ref, rw_ref, g_ref, b_ref, ys_hbm, x_ref, xb_ref, buf0, buf1, sem):
    bufs = (buf0, buf1)

    def row_copy(r, k):
        return pltpu.make_async_copy(ys_hbm.at[pl.ds(pos_ref[0, 2 * r + k], 1)], bufs[k].at[pl.ds(r, 1)], sem)

    _for_each_row(lambda r: (row_copy(r, 0).start(), row_copy(r, 1).start()))
    _for_each_row(lambda r: (row_copy(r, 0).wait(), row_copy(r, 1).wait()))
    f = rw_ref[:, 0:1] * buf0[...] + rw_ref[:, 1:2] * buf1[...]
    x = _layer_norm(DEEPNORM_ALPHA * h_ref[...] + f, g_ref[...], b_ref[...])
    x_ref[...] = x
    xb_ref[...] = x.astype(BF16)


def moe_combine(pos, h, rw, g, b, ys, name):
    row = lambda i: (i, 0)
    fix = lambda i: (0, 0)
    return pl.pallas_call(
        _combine_kernel,
        grid=(T_PAD // TILE,),
        in_specs=[pl.BlockSpec((None, 1, 2 * TILE), lambda i: (i, 0, 0), memory_space=pltpu.SMEM),
                  pl.BlockSpec((TILE, D_MODEL), row),
                  pl.BlockSpec((TILE, LANES), row),
                  pl.BlockSpec((1, D_MODEL), fix),
                  pl.BlockSpec((1, D_MODEL), fix),
                  pl.BlockSpec(memory_space=pl.ANY)],
        out_specs=[pl.BlockSpec((TILE, D_MODEL), row),
                   pl.BlockSpec((TILE, D_MODEL), row)],
        out_shape=[jax.ShapeDtypeStruct((T_PAD, D_MODEL), F32),
                   jax.ShapeDtypeStruct((T_PAD, D_MODEL), BF16)],
        scratch_shapes=[pltpu.VMEM((TILE, D_MODEL), F32),
                        pltpu.VMEM((TILE, D_MODEL), F32),
                        pltpu.SemaphoreType.DMA(())],
        compiler_params=_cparams(1, disable_bounds_checks=True),
        name=name,
    )(pos, h, rw, g, b, ys)


def sparse_moe(h, rw, ri, w_up, w_down, layer, g, b, tag):
    rank, cnt = moe_plan(ri, name=f"moe_plan_{tag}")
    counts = cnt[0, :N_EXPERTS]
    tiles = (counts + EXPERT_TILE - 1) // EXPERT_TILE
    tile_end = jnp.cumsum(tiles)
    tile_start = tile_end - tiles
    row_start = tile_start * EXPERT_TILE
    pos = jnp.take(row_start, ri[:, 0:2], axis=0) + rank[:, 0:2]
    pos = pos.reshape(T_PAD // TILE, 1, 2 * TILE)
    tile_ids = jnp.arange(MAX_EXPERT_TILES, dtype=jnp.int32)
    tile_expert = jnp.minimum(jnp.sum(tile_ids[:, None] >= tile_end[None, :], axis=1),
                              N_EXPERTS - 1).astype(jnp.int32)
    tile_first = jnp.any(tile_ids[:, None] == tile_start[None, :], axis=1).astype(jnp.int32)
    n_tiles = tile_end[-1:].astype(jnp.int32)
    fill_start = (row_start + counts).astype(jnp.int32)
    fill_len = (tiles * EXPERT_TILE - counts).astype(jnp.int32)
    xs = moe_dispatch(pos, fill_start, fill_len, n_tiles, h, name=f"moe_dispatch_{tag}")
    ys = moe_experts(tile_expert, tile_first, n_tiles, xs, w_up, w_down, layer, name=f"moe_experts_{tag}")
    return moe_combine(pos, h, rw, g, b, ys, name=f"moe_combine_{tag}")


def _softmax_pv(q, k, v, bias, valid):
    s = lax.dot_general(q, k, (((1,), (1,)), ((), ())), preferred_element_type=F32) * SCALE
    s = jnp.where(valid, s + bias, NEG_INF)
    m = jnp.max(s, axis=-1, keepdims=True)
    p = jnp.exp(s - m)
    den = jnp.sum(p, axis=-1, keepdims=True)
    pv = jnp.dot(p.astype(BF16), v, preferred_element_type=F32)
    return pv, m, den


def _attn_prompt_kernel(slopes_ref, q0, k0, v0, q1, k1, v1, q2, k2, v2, o_ref,
                        acc_o, acc_m, acc_d):
    hp = pl.program_id(1)
    q_refs, k_refs, v_refs = (q0, q1, q2), (k0, k1, k2), (v0, v1, v2)
    low = lax.broadcasted_iota(jnp.int32, (BAND, LANES), 1) < HEAD_DIM
    d2 = (lax.broadcasted_iota(jnp.int32, (BAND, 2 * BAND), 0) + BAND
          - lax.broadcasted_iota(jnp.int32, (BAND, 2 * BAND), 1))
    d1 = (lax.broadcasted_iota(jnp.int32, (BAND, BAND), 0)
          - lax.broadcasted_iota(jnp.int32, (BAND, BAND), 1))
    valid2 = (d2 >= 0) & (d2 <= BAND)
    valid1 = d1 >= 0
    d2f = d2.astype(F32)
    d1f = d1.astype(F32)
    slopes = (slopes_ref[2 * hp], slopes_ref[2 * hp + 1])

    for g, (_, dil) in enumerate(DIL_PAIRS):
        n_blk = SEQ // dil // BAND
        for r in range(dil):
            for b in range(n_blk):
                def rows(start, size):
                    return pl.ds(start, size, stride=dil) if dil > 1 else pl.ds(start, size)
                q_rows = rows(r + dil * BAND * b, BAND)
                if b == 0:
                    k_rows, distf, valid = rows(r, BAND), d1f, valid1
                else:
                    k_rows, distf, valid = rows(r + dil * BAND * (b - 1), 2 * BAND), d2f, valid2
                q = q_refs[g][q_rows, :]
                k = k_refs[g][k_rows, :].astype(BF16)
                v = v_refs[g][k_rows, :].astype(BF16)
                res = []
                for hh in range(2):
                    qm = jnp.where(low if hh == 0 else jnp.logical_not(low), q, 0.0).astype(BF16)
                    bias = distf * (-slopes[hh] * float(dil))
                    res.append(_softmax_pv(qm, k, v, bias, valid))
                acc_o[g, q_rows, :] = jnp.where(low, res[0][0], res[1][0])
                acc_m[g, q_rows, :] = jnp.where(low, res[0][1], res[1][1])
                acc_d[g, q_rows, :] = jnp.where(low, res[0][2], res[1][2])

    chunk = 256
    for c in range(SEQ // chunk):
        rws = pl.ds(c * chunk, chunk)
        m = [acc_m[g, rws, :] for g in range(N_DIL)]
        top = jnp.maximum(jnp.maximum(m[0], m[1]), m[2])
        num = jnp.zeros((chunk, LANES), F32)
        den = jnp.zeros((chunk, LANES), F32)
        for g in range(N_DIL):
            w = jnp.exp(m[g] - top)
            num = num + w * acc_o[g, rws, :]
            den = den + w * acc_d[g, rws, :]
        o_ref[rws, :] = (num / den).astype(o_ref.dtype)


def attn_prompt(proj, slopes):
    in_specs = [pl.BlockSpec(memory_space=pltpu.SMEM)]
    for g in range(N_DIL):
        for part in range(3):
            col0 = (g * 3 + part) * (D_MODEL // LANES)
            in_specs.append(pl.BlockSpec((SEQ, LANES),
                                         functools.partial(lambda n, hp, c: (n, c + hp), c=col0)))
    return pl.pallas_call(
        _attn_prompt_kernel,
        grid=(BATCH, N_HEADS // 2),
        in_specs=in_specs,
        out_specs=pl.BlockSpec((SEQ, LANES), lambda n, hp: (n, hp)),
        out_shape=jax.ShapeDtypeStruct((T_PROMPT, D_MODEL), BF16),
        scratch_shapes=[pltpu.VMEM((N_DIL, SEQ, LANES), F32)] * 3,
        compiler_params=_cparams(2),
        name="attn_prompt",
    )(slopes, *([proj] * 9))


N_A_LAYERS = 2


def _kv_transpose_kernel(a_ref, b_ref, o_ref):
    layer = pl.program_id(0)

    @pl.when(layer == 0)
    def _():
        o_ref[...] = a_ref[...].T

    @pl.when(layer == 1)
    def _():
        o_ref[...] = b_ref[...].T


def kv_transpose(proj_a, proj_b, g, win, name):
    rc = min(win, TILE)
    chunks = win // rc
    blocks_per_seq = SEQ // rc
    first = (SEQ - win) // rc

    def block(n, p, c):
        return (n * blocks_per_seq + first + c, g * 3 + 1 + p)

    a_last = block(BATCH - 1, 1, chunks - 1)
    b_first = block(0, 0, 0)

    def a_map(l, n, p, c):
        r, col = block(n, p, c)
        return (jnp.where(l == 0, r, a_last[0]), jnp.where(l == 0, col, a_last[1]))

    def b_map(l, n, p, c):
        r, col = block(n, p, c)
        return (jnp.where(l == 1, r, b_first[0]), jnp.where(l == 1, col, b_first[1]))

    return pl.pallas_call(
        _kv_transpose_kernel,
        grid=(N_A_LAYERS, BATCH, 2, chunks),
        in_specs=[pl.BlockSpec((rc, D_MODEL), a_map),
                  pl.BlockSpec((rc, D_MODEL), b_map)],
        out_specs=pl.BlockSpec((None, None, None, D_MODEL, rc), lambda l, n, p, c: (l, n, p, 0, c)),
        out_shape=jax.ShapeDtypeStruct((N_A_LAYERS, BATCH, 2, D_MODEL, win), F32),
        compiler_params=_cparams(4),
        name=name,
    )(proj_a, proj_b)


SAMPLE_HB = 8
SAMPLE_CHUNK = 512


def _attn_sample_kernel(slopes_ref, qkv_ref, c0_ref, c1_ref, c2_ref, o_ref):
    hb = pl.program_id(0)
    n = pl.program_id(1)
    caches = (c0_ref, c1_ref, c2_ref)

    @pl.when(n == 0)
    def _():
        o_ref[...] = jnp.zeros_like(o_ref)

    is_n = lax.broadcasted_iota(jnp.int32, (HEAD_DIM, DEC_BATCH), 1) == n

    def column(g, part, h):
        return jnp.sum(jnp.where(is_n, qkv_ref[g, part, h], 0.0), axis=1, keepdims=True)

    heads = range(SAMPLE_HB)
    row = lax.broadcasted_iota(jnp.int32, (SAMPLE_HB, 1), 0)
    neg_slope = jnp.zeros((SAMPLE_HB, 1), F32)
    for h in heads:
        neg_slope = jnp.where(row == h, -slopes_ref[hb * SAMPLE_HB + h], neg_slope)

    ms, nums, dens = [], [], []
    for g, (win, dil) in enumerate(DIL_PAIRS):
        chunk = min(win, SAMPLE_CHUNK)
        n_chunks = win // chunk
        q = [column(g, 0, h) for h in heads]
        s_new = jnp.concatenate([jnp.sum(q[h] * column(g, 1, h), axis=0, keepdims=True) for h in heads],
                                axis=0) * SCALE
        scores = []
        m = s_new
        for c in range(n_chunks):
            lanes = slice(c * chunk, (c + 1) * chunk)
            s = jnp.concatenate([jnp.sum(caches[g][0, h, :, lanes] * q[h], axis=0, keepdims=True)
                                 for h in heads], axis=0) * SCALE
            pos = c * chunk + lax.broadcasted_iota(jnp.int32, (1, chunk), 1)
            s = s + (win - pos).astype(F32) * neg_slope
            if dil > 1:
                s = jnp.where((pos & (dil - 1)) == 0, s, NEG_INF)
            scores.append(s)
            m = jnp.maximum(m, jnp.max(s, axis=1, keepdims=True))
        p_new = jnp.exp(s_new - m)
        den = p_new
        probs = []
        for s in scores:
            p = jnp.exp(s - m)
            den = den + jnp.sum(p, axis=1, keepdims=True)
            probs.append(p)
        group_nums = []
        for h in heads:
            acc = jnp.zeros((HEAD_DIM, chunk), F32)
            for c in range(n_chunks):
                acc = acc + caches[g][1, h, :, c * chunk:(c + 1) * chunk] * probs[c][h:h + 1, :]
            group_nums.append(jnp.sum(acc, axis=1, keepdims=True) + p_new[h:h + 1, :] * column(g, 2, h))
        nums.append(group_nums)
        dens.append(den)
        ms.append(m)

    top = jnp.maximum(jnp.maximum(ms[0], ms[1]), ms[2])
    weights = [jnp.exp(ms[g] - top) for g in range(N_DIL)]
    den = weights[0] * dens[0] + weights[1] * dens[1] + weights[2] * dens[2]
    for h in heads:
        num = jnp.zeros((HEAD_DIM, 1), F32)
        for g in range(N_DIL):
            num = num + weights[g][h:h + 1, :] * nums[g][h]
        o_ref[h] = jnp.where(is_n, num / den[h:h + 1, :], o_ref[h])


def attn_sample(qkv_t, caches, layer, slopes):
    hb = SAMPLE_HB
    in_specs = [pl.BlockSpec(memory_space=pltpu.SMEM),
                pl.BlockSpec((N_DIL, 3, hb, HEAD_DIM, DEC_BATCH), lambda b, n: (0, 0, b, 0, 0))]
    views = []
    for cache, (win, _) in zip(caches, DIL_PAIRS):
        views.append(jnp.transpose(cache, (0, 1, 3, 4, 5, 2)))
        in_specs.append(pl.BlockSpec((None, None, 2, hb, HEAD_DIM, win),
                                     lambda b, n: (layer, n, 0, b, 0, 0)))
    return pl.pallas_call(
        _attn_sample_kernel,
        grid=(N_HEADS // hb, DEC_BATCH),
        in_specs=in_specs,
        out_specs=pl.BlockSpec((hb, HEAD_DIM, DEC_BATCH), lambda b, n: (b, 0, 0)),
        out_shape=jax.ShapeDtypeStruct((N_HEADS, HEAD_DIM, DEC_BATCH), F32),
        compiler_params=_cparams(2),
        name="attn_sample",
    )(slopes, qkv_t, *views)


HIST = 16


def _pool_prompt_kernel(u_ref, wg_ref, sc_ref, z_ref, ext_ref, wgb_ref):
    i = pl.program_id(0)
    tiles_per_seq = SEQ // TILE

    @pl.when(i == 0)
    def _():
        wgb_ref[...] = wg_ref[...].astype(BF16)

    @pl.when(i % tiles_per_seq == 0)
    def _():
        ext_ref[0:HIST, :] = jnp.zeros((HIST, D_MODEL), F32)

    ext_ref[HIST:HIST + TILE, :] = u_ref[...]
    pos = (i % tiles_per_seq) * TILE + lax.broadcasted_iota(jnp.int32, (TILE, 1), 0)
    for g, win in enumerate(POOL_WINDOWS):
        cols = slice(g * POOL_CH, (g + 1) * POOL_CH)
        acc = ext_ref[HIST:HIST + TILE, cols]
        for k in range(1, win):
            acc = acc + ext_ref[HIST - k:HIST - k + TILE, cols]
        cnt = jnp.minimum(win, pos + 1).astype(F32)
        pooled = acc / cnt - u_ref[:, cols]
        z = jnp.dot(pooled.astype(BF16), wgb_ref[g], preferred_element_type=F32)
        z_ref[:, cols] = (z * sc_ref[:, cols]).astype(z_ref.dtype)
    ext_ref[0:HIST, :] = ext_ref[TILE:TILE + HIST, :]


def pool_prompt(u, w_group, scale):
    return pl.pallas_call(
        _pool_prompt_kernel,
        grid=(T_PROMPT // TILE,),
        in_specs=[pl.BlockSpec((TILE, D_MODEL), lambda i: (i, 0)),
                  pl.BlockSpec((None, len(POOL_WINDOWS), POOL_CH, POOL_CH), lambda i: (0, 0, 0, 0)),
                  pl.BlockSpec((1, D_MODEL), lambda i: (0, 0))],
        out_specs=pl.BlockSpec((TILE, D_MODEL), lambda i: (i, 0)),
        out_shape=jax.ShapeDtypeStruct((T_PROMPT, D_MODEL), BF16),
        scratch_shapes=[pltpu.VMEM((HIST + TILE, D_MODEL), F32),
                        pltpu.VMEM((len(POOL_WINDOWS), POOL_CH, POOL_CH), BF16)],
        compiler_params=_cparams(1),
        name="pool_prompt",
    )(u, w_group, scale)


def _pool_sample_kernel(u_ref, st_ref, wg_ref, sc_ref, z_ref):
    for g, win in enumerate(POOL_WINDOWS):
        cols = slice(g * POOL_CH, (g + 1) * POOL_CH)
        u = u_ref[:, cols]
        acc = u
        for k in range(1, win):
            acc = acc + st_ref[POOL_STATE - k, :, cols]
        cnt = float(min(win, PAST_LEN + 1))
        pooled = acc / cnt - u
        z = jnp.dot(pooled.astype(BF16), wg_ref[g].astype(BF16), preferred_element_type=F32)
        z_ref[:, cols] = (z * sc_ref[:, cols]).astype(z_ref.dtype)


def pool_sample(u, state_t, w_group, scale):
    return pl.pallas_call(
        _pool_sample_kernel,
        grid=(1,),
        in_specs=[pl.BlockSpec((DEC_BATCH, D_MODEL), lambda i: (0, 0)),
                  pl.BlockSpec((POOL_STATE, DEC_BATCH, D_MODEL), lambda i: (0, 0, 0)),
                  pl.BlockSpec((None, len(POOL_WINDOWS), POOL_CH, POOL_CH), lambda i: (0, 0, 0, 0)),
                  pl.BlockSpec((1, D_MODEL), lambda i: (0, 0))],
        out_specs=pl.BlockSpec((DEC_BATCH, D_MODEL), lambda i: (0, 0)),
        out_shape=jax.ShapeDtypeStruct((DEC_BATCH, D_MODEL), BF16),
        compiler_params=_cparams(1),
        name="pool_sample",
    )(u, state_t, w_group, scale)


CONV_HIST = 8


def _conv_prompt_kernel(p_ref, cw_ref, y_ref, ch_ref, ext_ref):
    i = pl.program_id(0)

    @pl.when(i % (SEQ // TILE) == 0)
    def _():
        ext_ref[0:CONV_HIST, :] = jnp.zeros((CONV_HIST, D_MODEL), F32)

    ch = p_ref[:, D_MODEL:2 * D_MODEL] * p_ref[:, 2 * D_MODEL:]
    ext_ref[CONV_HIST:CONV_HIST + TILE, :] = ch
    conv = cw_ref[0:1, :] * ext_ref[CONV_HIST - 2:CONV_HIST - 2 + TILE, :]
    conv = conv + cw_ref[1:2, :] * ext_ref[CONV_HIST - 1:CONV_HIST - 1 + TILE, :]
    conv = conv + cw_ref[2:3, :] * ch
    y_ref[...] = (p_ref[:, :D_MODEL] * conv).astype(y_ref.dtype)
    ch_ref[...] = ch
    ext_ref[0:CONV_HIST, :] = ext_ref[TILE:TILE + CONV_HIST, :]


def conv_prompt(p, conv_w):
    return pl.pallas_call(
        _conv_prompt_kernel,
        grid=(T_PROMPT // TILE,),
        in_specs=[pl.BlockSpec((TILE, 3 * D_MODEL), lambda i: (i, 0)),
                  pl.BlockSpec((None, 3, D_MODEL), lambda i: (0, 0, 0))],
        out_specs=[pl.BlockSpec((TILE, D_MODEL), lambda i: (i, 0)),
                   pl.BlockSpec((TILE, D_MODEL), lambda i: (i, 0))],
        out_shape=[jax.ShapeDtypeStruct((T_PROMPT, D_MODEL), BF16),
                   jax.ShapeDtypeStruct((T_PROMPT, D_MODEL), F32)],
        scratch_shapes=[pltpu.VMEM((CONV_HIST + TILE, D_MODEL), F32)],
        compiler_params=_cparams(1),
        name="conv_prompt",
    )(p, conv_w)


def _conv_sample_kernel(p_ref, st_ref, cw_ref, y_ref, ch_ref):
    ch = p_ref[:, D_MODEL:2 * D_MODEL] * p_ref[:, 2 * D_MODEL:]
    conv = cw_ref[0:1, :] * st_ref[0]
    conv = conv + cw_ref[1:2, :] * st_ref[1]
    conv = conv + cw_ref[2:3, :] * ch
    y_ref[...] = (p_ref[:, :D_MODEL] * conv).astype(y_ref.dtype)
    ch_ref[...] = ch


def conv_sample(p, state_t, conv_w):
    return pl.pallas_call(
        _conv_sample_kernel,
        grid=(1,),
        in_specs=[pl.BlockSpec((DEC_BATCH, 3 * D_MODEL), lambda i: (0, 0)),
                  pl.BlockSpec((2, DEC_BATCH, D_MODEL), lambda i: (0, 0, 0)),
                  pl.BlockSpec((None, 3, D_MODEL), lambda i: (0, 0, 0))],
        out_specs=[pl.BlockSpec((DEC_BATCH, D_MODEL), lambda i: (0, 0)),
                   pl.BlockSpec((DEC_BATCH, D_MODEL), lambda i: (0, 0))],
        out_shape=[jax.ShapeDtypeStruct((DEC_BATCH, D_MODEL), BF16),
                   jax.ShapeDtypeStruct((DEC_BATCH, D_MODEL), F32)],
        compiler_params=_cparams(1),
        name="conv_sample",
    )(p, state_t, conv_w)


def _tail_rows(sample_rows):
    return jnp.pad(sample_rows.astype(BF16), ((0, TAIL - DEC_BATCH), (0, 0)))


def kernel(x_prompt, x_sample, cache_kv_w128, cache_kv_w512, cache_kv_w2048, state_pool, state_conv, a_w_in, a_w_out, b_w_in, b_w_group, b_scale, b_w_out, c_w_in, c_conv, c_w_out, ln1_g, ln1_b, ln2_g, ln2_b, moe_w_group, moe_b_group, moe_w_expert, moe_b_expert, moe_w_up, moe_w_down):
    caches = (cache_kv_w128, cache_kv_w512, cache_kv_w2048)
    heads = jnp.arange(1, N_HEADS + 1, dtype=F32)
    slopes = jnp.exp2(-8.0 * heads / N_HEADS)

    x = jnp.concatenate([x_prompt.reshape(T_PROMPT, D_MODEL),
                         x_sample.reshape(DEC_BATCH, D_MODEL),
                         jnp.zeros((T_PAD - T_REAL, D_MODEL), F32)], axis=0)
    xb = x.astype(BF16)

    projs = []
    kv_s = [[] for _ in DIL_PAIRS]
    pool_p = pool_s = conv_p = conv_s = None

    for i in range(DEPTH):
        j, kind = i // N_MIXERS, i % N_MIXERS
        if kind == 0:
            proj, proj_t = project(xb, a_w_in, j, 1536, name=f"a_in_{i}")
            o = attn_prompt(proj, slopes)
            qkv_s = proj_t[:DEC_BATCH].reshape(DEC_BATCH, N_DIL, 3, N_HEADS, HEAD_DIM)
            o_t = attn_sample(jnp.transpose(qkv_s, (1, 2, 3, 4, 0)), caches, j, slopes)
            mixed, tail = o, _tail_rows(jnp.transpose(o_t, (2, 0, 1)).reshape(DEC_BATCH, D_MODEL))
            w_out, layer_w = a_w_out, j
            projs.append(proj)
            for g in range(N_DIL):
                kv_s[g].append(qkv_s[:, None, g, 1:3])
        elif kind == 1:
            u, u_t = project(xb, b_w_in, j, 1024, name=f"b_in_{i}")
            z = pool_prompt(u, b_w_group[j:j + 1], b_scale[j:j + 1])
            state_t = jnp.swapaxes(state_pool[j], 0, 1)
            z_s = pool_sample(u_t, state_t, b_w_group[j:j + 1], b_scale[j:j + 1])
            mixed, tail = z, _tail_rows(z_s)
            w_out, layer_w = b_w_out, j
            u_p = u.reshape(BATCH, SEQ, D_MODEL)
            u_s = u_t[:DEC_BATCH]
            pool_p = u_p[:, SEQ - POOL_STATE:][None]
            pool_s = jnp.concatenate([state_pool[j][:, 1:], u_s[:, None]], axis=1)[None]
        else:
            p3, p3_t = project(xb, c_w_in, j, 1536, name=f"c_in_{i}")
            y, ch = conv_prompt(p3, c_conv[j:j + 1])
            state_t = jnp.swapaxes(state_conv[j], 0, 1)
            y_s, ch_s = conv_sample(p3_t, state_t, c_conv[j:j + 1])
            mixed, tail = y, _tail_rows(y_s)
            w_out, layer_w = c_w_out, j
            conv_p = ch.reshape(BATCH, SEQ, D_MODEL)[:, SEQ - 2:][None]
            conv_s = jnp.concatenate([state_conv[j][:, 1:], ch_s[:, None]], axis=1)[None]

        wr = jnp.concatenate(
            [moe_w_group[i],
             jnp.transpose(moe_w_expert[i], (1, 0, 2)).reshape(D_MODEL, N_EXPERTS),
             jnp.zeros((D_MODEL, LANES - N_GROUPS - N_EXPERTS), F32)], axis=1)
        br = jnp.concatenate([moe_b_group[i], moe_b_expert[i].reshape(N_EXPERTS),
                              jnp.zeros((LANES - N_GROUPS - N_EXPERTS,), F32)])[None]
        h, hb, rw, ri = mix_out(mixed, tail, w_out, layer_w, x, ln1_g[i:i + 1], ln1_b[i:i + 1],
                                wr, br, name=f"mix_out_{i}")
        x, xb = sparse_moe(h, rw, ri, moe_w_up, moe_w_down, i, ln2_g[i:i + 1], ln2_b[i:i + 1], i)

    y_prompt = x[:T_PROMPT].reshape(BATCH, SEQ, D_MODEL)
    y_sample = x[T_PROMPT:T_REAL].reshape(DEC_BATCH, 1, D_MODEL)
    kv_p = []
    for g, (win, _) in enumerate(DIL_PAIRS):
        kv_t = kv_transpose(projs[0], projs[1], g, win, name=f"kv_out_{g}")
        kv_t = kv_t.reshape(N_A_LAYERS, BATCH, 2, N_HEADS, HEAD_DIM, win)
        kv_p.append(jnp.transpose(kv_t, (0, 1, 5, 2, 3, 4)))
    stack = lambda rows: jnp.stack(rows)
    return (y_prompt, y_sample,
            kv_p[0], stack(kv_s[0]), kv_p[1], stack(kv_s[1]),
            kv_p[2], stack(kv_s[2]), pool_p, pool_s, conv_p, conv_s)
```

```python
import functools

import jax
import jax.numpy as jnp
from jax import lax
from jax.experimental import pallas as pl
from jax.experimental.pallas import tpu as pltpu

F32 = jnp.float32
BF16 = jnp.bfloat16

D_MODEL = 1024
BATCH = 8
SEQ = 2048
DEPTH = 4
DEC_BATCH = 128
PAST_LEN = 2048
N_MIXERS = 3
N_HEADS = 16
HEAD_DIM = 64
DIL_PAIRS = ((128, 1), (512, 4), (2048, 16))
N_DIL = 3
BAND = 128
POOL_WINDOWS = (2, 4, 8, 16)
POOL_CH = 256
POOL_STATE = 15
N_GROUPS = 4
EXP_PER_GROUP = 8
N_EXPERTS = 32
EXPERT_FF = 256
DEEPNORM_ALPHA = (2 * DEPTH) ** 0.25
LN_EPS = 1e-5
NEG_INF = -1e30
SCALE = HEAD_DIM ** -0.5

T_PROMPT = BATCH * SEQ
T_REAL = T_PROMPT + DEC_BATCH
TILE = 512
T_PAD = 16896
TAIL = T_PAD - T_PROMPT
LANES = 128
VMEM_LIMIT = 56 * 1024 * 1024


def _cparams(n_axes, **kwargs):
    return pltpu.CompilerParams(dimension_semantics=("arbitrary",) * n_axes,
                                vmem_limit_bytes=VMEM_LIMIT, **kwargs)


def _mm_kernel(x_ref, w_ref, o_ref, wb_ref):
    @pl.when(pl.program_id(1) == 0)
    def _():
        wb_ref[...] = w_ref[...].astype(BF16)

    o_ref[...] = jnp.dot(x_ref[...], wb_ref[...],
                         preferred_element_type=F32).astype(o_ref.dtype)


def matmul(x, w, layer, row0, rows, tn, tm, name):
    k = x.shape[1]
    n = w.shape[-1]
    blk0 = row0 // tm
    return pl.pallas_call(
        _mm_kernel,
        grid=(n // tn, rows // tm),
        in_specs=[pl.BlockSpec((tm, k), lambda j, i: (blk0 + i, 0)),
                  pl.BlockSpec((None, k, tn), lambda j, i: (layer, 0, j))],
        out_specs=pl.BlockSpec((tm, tn), lambda j, i: (i, j)),
        out_shape=jax.ShapeDtypeStruct((rows, n), F32),
        scratch_shapes=[pltpu.VMEM((k, tn), BF16)],
        compiler_params=_cparams(2),
        name=name,
    )(x, w)


def project(xb, w, layer, tn_tail, name):
    prompt = matmul(xb, w, layer, 0, T_PROMPT, 1024, SEQ, name + "_p")
    tail = matmul(xb, w, layer, T_PROMPT, TAIL, tn_tail, TAIL, name + "_t")
    return prompt, tail


def _layer_norm(z, g, b):
    mu = jnp.mean(z, axis=-1, keepdims=True)
    zc = z - mu
    var = jnp.mean(zc * zc, axis=-1, keepdims=True)
    return zc * lax.rsqrt(var + LN_EPS) * g + b


def _route(logits):
    lane = lax.broadcasted_iota(jnp.int32, logits.shape, 1)
    gl = jnp.where(lane < N_GROUPS, logits, -jnp.inf)
    gmax = jnp.max(gl, axis=-1, keepdims=True)
    gsel = jnp.min(jnp.where(gl == gmax, lane, LANES), axis=-1, keepdims=True)
    p_g = 1.0 / jnp.sum(jnp.exp(gl - gmax), axis=-1, keepdims=True)
    lo = N_GROUPS + EXP_PER_GROUP * gsel
    el = jnp.where((lane >= lo) & (lane < lo + EXP_PER_GROUP), logits, -jnp.inf)
    v0 = jnp.max(el, axis=-1, keepdims=True)
    i0 = jnp.min(jnp.where(el == v0, lane, LANES), axis=-1, keepdims=True)
    el = jnp.where(lane == i0, -jnp.inf, el)
    v1 = jnp.max(el, axis=-1, keepdims=True)
    i1 = jnp.min(jnp.where(el == v1, lane, LANES), axis=-1, keepdims=True)
    e = jnp.exp(v1 - v0)
    q0 = 1.0 / (1.0 + e)
    q1 = e / (1.0 + e)
    rw = jnp.where(lane == 0, p_g * q0, jnp.where(lane == 1, p_g * q1, 0.0))
    ri = jnp.where(lane == 0, i0 - N_GROUPS, jnp.where(lane == 1, i1 - N_GROUPS, 0))
    return rw, ri


def _mix_out_kernel(a_ref, at_ref, w_ref, x_ref, g_ref, b_ref, wr_ref, br_ref,
                    h_ref, hb_ref, rw_ref, ri_ref, wb_ref, wrb_ref):
    i = pl.program_id(0)

    @pl.when(i == 0)
    def _():
        wb_ref[...] = w_ref[...].astype(BF16)
        wr = wr_ref[...]
        wr_hi = wr.astype(BF16)
        wrb_ref[:, :LANES] = wr_hi
        wrb_ref[:, LANES:] = (wr - wr_hi.astype(F32)).astype(BF16)

    a = jnp.where(i < T_PROMPT // TILE, a_ref[...], at_ref[...])
    y = jnp.dot(a, wb_ref[...], preferred_element_type=F32)
    h = _layer_norm(DEEPNORM_ALPHA * x_ref[...] + y, g_ref[...], b_ref[...])
    hb = h.astype(BF16)
    h_ref[...] = h
    hb_ref[...] = hb
    h_lo = (h - hb.astype(F32)).astype(BF16)
    split = jnp.dot(hb, wrb_ref[...], preferred_element_type=F32)
    logits = (split[:, :LANES] + split[:, LANES:]
              + jnp.dot(h_lo, wrb_ref[:, :LANES], preferred_element_type=F32) + br_ref[...])
    rw, ri = _route(logits)
    rw_ref[...] = rw
    ri_ref[...] = ri


def mix_out(a, a_tail, w_out, layer_w, x, g, b, wr, br, name):
    row = lambda i: (i, 0)
    fix = lambda i: (0, 0)
    return pl.pallas_call(
        _mix_out_kernel,
        grid=(T_PAD // TILE,),
        in_specs=[pl.BlockSpec((TILE, D_MODEL), lambda i: (jnp.minimum(i, T_PROMPT // TILE - 1), 0)),
                  pl.BlockSpec((TAIL, D_MODEL), fix),
                  pl.BlockSpec((None, D_MODEL, D_MODEL), lambda i: (layer_w, 0, 0)),
                  pl.BlockSpec((TILE, D_MODEL), row),
                  pl.BlockSpec((1, D_MODEL), fix),
                  pl.BlockSpec((1, D_MODEL), fix),
                  pl.BlockSpec((D_MODEL, LANES), fix),
                  pl.BlockSpec((1, LANES), fix)],
        out_specs=[pl.BlockSpec((TILE, D_MODEL), row),
                   pl.BlockSpec((TILE, D_MODEL), row),
                   pl.BlockSpec((TILE, LANES), row),
                   pl.BlockSpec((TILE, LANES), row)],
        out_shape=[jax.ShapeDtypeStruct((T_PAD, D_MODEL), F32),
                   jax.ShapeDtypeStruct((T_PAD, D_MODEL), BF16),
                   jax.ShapeDtypeStruct((T_PAD, LANES), F32),
                   jax.ShapeDtypeStruct((T_PAD, LANES), jnp.int32)],
        scratch_shapes=[pltpu.VMEM((D_MODEL, D_MODEL), BF16),
                        pltpu.VMEM((D_MODEL, 2 * LANES), BF16)],
        compiler_params=_cparams(1),
        name=name,
    )(a, a_tail, w_out, x, g, b, wr, br)


EXPERT_TILE = 512
N_ASSIGN = 2 * T_PAD
MAX_EXPERT_TILES = (N_ASSIGN + N_EXPERTS * (EXPERT_TILE - 1)) // EXPERT_TILE
SORTED_ROWS = MAX_EXPERT_TILES * EXPERT_TILE


def _plan_kernel(ri_ref, rank_ref, cnt_ref, base_ref):
    i = pl.program_id(0)

    @pl.when(i == 0)
    def _():
        base_ref[...] = jnp.zeros_like(base_ref)

    lane = lax.broadcasted_iota(jnp.int32, (TILE, LANES), 1)
    e0 = ri_ref[:, 0:1]
    e1 = ri_ref[:, 1:2]
    hit0 = lane == e0
    hit1 = lane == e1
    onehot = jnp.where(hit0 | hit1, 1.0, 0.0)
    r = lax.broadcasted_iota(jnp.int32, (TILE, TILE), 0)
    c = lax.broadcasted_iota(jnp.int32, (TILE, TILE), 1)
    earlier = jnp.where(c < r, 1.0, 0.0).astype(BF16)
    before = jnp.dot(earlier, onehot.astype(BF16), preferred_element_type=F32) + base_ref[...]
    rank0 = jnp.sum(jnp.where(hit0, before, 0.0), axis=-1, keepdims=True)
    rank1 = jnp.sum(jnp.where(hit1, before, 0.0), axis=-1, keepdims=True)
    rank_ref[...] = jnp.where(lane == 0, rank0, jnp.where(lane == 1, rank1, 0.0)).astype(jnp.int32)
    base_ref[...] = base_ref[...] + jnp.sum(onehot, axis=0, keepdims=True)
    cnt_ref[...] = jnp.broadcast_to(base_ref[...], cnt_ref.shape).astype(jnp.int32)


def moe_plan(ri, name):
    return pl.pallas_call(
        _plan_kernel,
        grid=(T_PAD // TILE,),
        in_specs=[pl.BlockSpec((TILE, LANES), lambda i: (i, 0))],
        out_specs=[pl.BlockSpec((TILE, LANES), lambda i: (i, 0)),
                   pl.BlockSpec((8, LANES), lambda i: (0, 0))],
        out_shape=[jax.ShapeDtypeStruct((T_PAD, LANES), jnp.int32),
                   jax.ShapeDtypeStruct((8, LANES), jnp.int32)],
        scratch_shapes=[pltpu.VMEM((1, LANES), F32)],
        compiler_params=_cparams(1),
        name=name,
    )(ri)


ROW_UNROLL = 8


def _for_each_row(body):
    def step(c, carry):
        for k in range(ROW_UNROLL):
            body(c * ROW_UNROLL + k)
        return carry

    lax.fori_loop(0, TILE // ROW_UNROLL, step, 0)


SUBLANES = 8
FILL_PIECES = tuple(2 ** k for k in range(EXPERT_TILE.bit_length() - 2, 2, -1))


def _zero_fill(fill_start_ref, fill_len_ref, n_tiles_ref, zeros_ref, xs_hbm, sem, wait):
    def run(copy):
        copy.wait() if wait else copy.start()

    for e in range(N_EXPERTS):
        start = fill_start_ref[e]
        head = (-start) & (SUBLANES - 1)
        for k in range(SUBLANES - 1):
            @pl.when(k < head)
            def _(k=k):
                run(pltpu.make_async_copy(zeros_ref.at[pl.ds(0, 1)], xs_hbm.at[pl.ds(start + k, 1)], sem))
        row = start + head
        length = fill_len_ref[e] - head
        for piece in FILL_PIECES:
            @pl.when((length & piece) != 0)
            def _(row=row, piece=piece):
                run(pltpu.make_async_copy(zeros_ref.at[pl.ds(0, piece)],
                                          xs_hbm.at[pl.ds(pl.multiple_of(row, SUBLANES), piece)], sem))
            row = row + (length & piece)

    def tail(t, carry):
        run(pltpu.make_async_copy(zeros_ref, xs_hbm.at[pl.ds(t * EXPERT_TILE, EXPERT_TILE)], sem))
        return carry

    lax.fori_loop(n_tiles_ref[0], MAX_EXPERT_TILES, tail, 0)


def _dispatch_kernel(pos_ref, fill_start_ref, fill_len_ref, n_tiles_ref, h_ref, xs_hbm,
                     zeros_ref, sem, fill_sem):
    i = pl.program_id(0)

    @pl.when(i == 0)
    def _():
        zeros_ref[...] = jnp.zeros_like(zeros_ref)
        _zero_fill(fill_start_ref, fill_len_ref, n_tiles_ref, zeros_ref, xs_hbm, fill_sem, wait=False)

    def row_copy(r, k):
        return pltpu.make_async_copy(h_ref.at[pl.ds(r, 1)], xs_hbm.at[pl.ds(pos_ref[0, 2 * r + k], 1)], sem)

    _for_each_row(lambda r: (row_copy(r, 0).start(), row_copy(r, 1).start()))
    _for_each_row(lambda r: (row_copy(r, 0).wait(), row_copy(r, 1).wait()))

    @pl.when(i == 0)
    def _():
        _zero_fill(fill_start_ref, fill_len_ref, n_tiles_ref, zeros_ref, xs_hbm, fill_sem, wait=True)


def moe_dispatch(pos, fill_start, fill_len, n_tiles, h, name):
    smem = pl.BlockSpec(memory_space=pltpu.SMEM)
    return pl.pallas_call(
        _dispatch_kernel,
        grid=(T_PAD // TILE,),
        in_specs=[pl.BlockSpec((None, 1, 2 * TILE), lambda i: (i, 0, 0), memory_space=pltpu.SMEM),
                  smem, smem, smem,
                  pl.BlockSpec((TILE, D_MODEL), lambda i: (i, 0))],
        out_specs=pl.BlockSpec(memory_space=pl.ANY),
        out_shape=jax.ShapeDtypeStruct((SORTED_ROWS, D_MODEL), F32),
        scratch_shapes=[pltpu.VMEM((EXPERT_TILE, D_MODEL), F32),
                        pltpu.SemaphoreType.DMA(()),
                        pltpu.SemaphoreType.DMA(())],
        compiler_params=pltpu.CompilerParams(dimension_semantics=("arbitrary",),
                                             has_side_effects=True, disable_bounds_checks=True),
        name=name,
    )(pos, fill_start, fill_len, n_tiles, h)


def _expert_kernel(tile_expert_ref, tile_first_ref, n_tiles_ref, xs_ref, wu_ref, wd_ref, ys_ref,
                   wub_ref, wdb_ref):
    t = pl.program_id(0)

    @pl.when(t < n_tiles_ref[0])
    def _():
        @pl.when(tile_first_ref[t] == 1)
        def _():
            wub_ref[...] = wu_ref[...].astype(BF16)
            wdb_ref[...] = wd_ref[...].astype(BF16)

        up = jnp.dot(xs_ref[...].astype(BF16), wub_ref[...], preferred_element_type=F32)
        a = up[:, :EXPERT_FF]
        u = up[:, EXPERT_FF:]
        hid = (a / (1.0 + jnp.exp(-a))) * u
        ys_ref[...] = jnp.dot(hid.astype(BF16), wdb_ref[...], preferred_element_type=F32)

    @pl.when(t >= n_tiles_ref[0])
    def _():
        ys_ref[...] = jnp.zeros_like(ys_ref)


def moe_experts(tile_expert, tile_first, n_tiles, xs, w_up, w_down, layer, name):
    def row_map(t, te, tf, nt):
        return (jnp.minimum(t, nt[0] - 1), 0)

    def out_map(t, te, tf, nt):
        return (t, 0)

    def w_map(t, te, tf, nt):
        return (layer, te[jnp.minimum(t, nt[0] - 1)], 0, 0)

    grid_spec = pltpu.PrefetchScalarGridSpec(
        num_scalar_prefetch=3,
        grid=(MAX_EXPERT_TILES,),
        in_specs=[pl.BlockSpec((EXPERT_TILE, D_MODEL), row_map),
                  pl.BlockSpec((None, None, D_MODEL, 2 * EXPERT_FF), w_map),
                  pl.BlockSpec((None, None, EXPERT_FF, D_MODEL), w_map)],
        out_specs=pl.BlockSpec((EXPERT_TILE, D_MODEL), out_map),
        scratch_shapes=[pltpu.VMEM((D_MODEL, 2 * EXPERT_FF), BF16),
                        pltpu.VMEM((EXPERT_FF, D_MODEL), BF16)])
    return pl.pallas_call(
        _expert_kernel,
        grid_spec=grid_spec,
        out_shape=jax.ShapeDtypeStruct((SORTED_ROWS, D_MODEL), F32),
        compiler_params=_cparams(1),
        name=name,
    )(tile_expert, tile_first, n_tiles, xs, w_up, w_down)


def _combine_kernel(pos_ref, h_ref, rw_ref, g_ref, b_ref, ys_hbm, x_ref, xb_ref, buf0, buf1, sem):
    bufs = (buf0, buf1)

    def row_copy(r, k):
        return pltpu.make_async_copy(ys_hbm.at[pl.ds(pos_ref[0, 2 * r + k], 1)], bufs[k].at[pl.ds(r, 1)], sem)

    _for_each_row(lambda r: (row_copy(r, 0).start(), row_copy(r, 1).start()))
    _for_each_row(lambda r: (row_copy(r, 0).wait(), row_copy(r, 1).wait()))
    f = rw_ref[:, 0:1] * buf0[...] + rw_ref[:, 1:2] * buf1[...]
    x = _layer_norm(DEEPNORM_ALPHA * h_ref[...] + f, g_ref[...], b_ref[...])
    x_ref[...] = x
    xb_ref[...] = x.astype(BF16)


def moe_combine(pos, h, rw, g, b, ys, name):
    row = lambda i: (i, 0)
    fix = lambda i: (0, 0)
    return pl.pallas_call(
        _combine_kernel,
        grid=(T_PAD // TILE,),
        in_specs=[pl.BlockSpec((None, 1, 2 * TILE), lambda i: (i, 0, 0), memory_space=pltpu.SMEM),
                  pl.BlockSpec((TILE, D_MODEL), row),
                  pl.BlockSpec((TILE, LANES), row),
                  pl.BlockSpec((1, D_MODEL), fix),
                  pl.BlockSpec((1, D_MODEL), fix),
                  pl.BlockSpec(memory_space=pl.ANY)],
        out_specs=[pl.BlockSpec((TILE, D_MODEL), row),
                   pl.BlockSpec((TILE, D_MODEL), row)],
        out_shape=[jax.ShapeDtypeStruct((T_PAD, D_MODEL), F32),
                   jax.ShapeDtypeStruct((T_PAD, D_MODEL), BF16)],
        scratch_shapes=[pltpu.VMEM((TILE, D_MODEL), F32),
                        pltpu.VMEM((TILE, D_MODEL), F32),
                        pltpu.SemaphoreType.DMA(())],
        compiler_params=_cparams(1, disable_bounds_checks=True),
        name=name,
    )(pos, h, rw, g, b, ys)


def sparse_moe(h, rw, ri, w_up, w_down, layer, g, b, tag):
    rank, cnt = moe_plan(ri, name=f"moe_plan_{tag}")
    counts = cnt[0, :N_EXPERTS]
    tiles = (counts + EXPERT_TILE - 1) // EXPERT_TILE
    tile_end = jnp.cumsum(tiles)
    tile_start = tile_end - tiles
    row_start = tile_start * EXPERT_TILE
    pos = jnp.take(row_start, ri[:, 0:2], axis=0) + rank[:, 0:2]
    pos = pos.reshape(T_PAD // TILE, 1, 2 * TILE)
    tile_ids = jnp.arange(MAX_EXPERT_TILES, dtype=jnp.int32)
    tile_expert = jnp.minimum(jnp.sum(tile_ids[:, None] >= tile_end[None, :], axis=1),
                              N_EXPERTS - 1).astype(jnp.int32)
    tile_first = jnp.any(tile_ids[:, None] == tile_start[None, :], axis=1).astype(jnp.int32)
    n_tiles = tile_end[-1:].astype(jnp.int32)
    fill_start = (row_start + counts).astype(jnp.int32)
    fill_len = (tiles * EXPERT_TILE - counts).astype(jnp.int32)
    xs = moe_dispatch(pos, fill_start, fill_len, n_tiles, h, name=f"moe_dispatch_{tag}")
    ys = moe_experts(tile_expert, tile_first, n_tiles, xs, w_up, w_down, layer, name=f"moe_experts_{tag}")
    return moe_combine(pos, h, rw, g, b, ys, name=f"moe_combine_{tag}")


UNITS_PER_GROUP = SEQ // BAND


def _attn_prompt_kernel(slopes_ref, q0, k0, v0, q1, k1, v1, q2, k2, v2, o_ref,
                        acc_o, acc_m, acc_d, s_scr, p_scr):
    hp = pl.program_id(1)
    q_refs, k_refs, v_refs = (q0, q1, q2), (k0, k1, k2), (v0, v1, v2)
    low = lax.broadcasted_iota(jnp.int32, (BAND, LANES), 1) < HEAD_DIM
    d2 = (lax.broadcasted_iota(jnp.int32, (BAND, 2 * BAND), 0) + BAND
          - lax.broadcasted_iota(jnp.int32, (BAND, 2 * BAND), 1))
    d1 = (lax.broadcasted_iota(jnp.int32, (BAND, BAND), 0)
          - lax.broadcasted_iota(jnp.int32, (BAND, BAND), 1))
    valid2 = (d2 >= 0) & (d2 <= BAND)
    valid1 = d1 >= 0
    d2f = d2.astype(F32)
    d1f = d1.astype(F32)
    slopes = (slopes_ref[2 * hp], slopes_ref[2 * hp + 1])

    for g, (_, dil) in enumerate(DIL_PAIRS):
        n_blk = SEQ // dil // BAND
        units = [(r, b) for r in range(dil) for b in range(n_blk)]

        def rows(start, size):
            return pl.ds(start, size, stride=dil) if dil > 1 else pl.ds(start, size)

        def q_rows(r, b):
            return rows(r + dil * BAND * b, BAND)

        def k_rows(r, b):
            return rows(r, BAND) if b == 0 else rows(r + dil * BAND * (b - 1), 2 * BAND)

        for u, (r, b) in enumerate(units):
            nk = BAND if b == 0 else 2 * BAND
            q = q_refs[g][q_rows(r, b), :]
            k = k_refs[g][k_rows(r, b), :].astype(BF16)
            for hh in range(2):
                qm = jnp.where(low if hh == 0 else jnp.logical_not(low), q, 0.0).astype(BF16)
                s_scr[2 * u + hh, :, :nk] = lax.dot_general(qm, k, (((1,), (1,)), ((), ())),
                                                            preferred_element_type=F32)
        for u, (r, b) in enumerate(units):
            nk = BAND if b == 0 else 2 * BAND
            distf, valid = (d1f, valid1) if b == 0 else (d2f, valid2)
            maxes, dens = [], []
            for hh in range(2):
                s = s_scr[2 * u + hh, :, :nk] * SCALE
                s = jnp.where(valid, s + distf * (-slopes[hh] * float(dil)), NEG_INF)
                m = jnp.max(s, axis=-1, keepdims=True)
                p = jnp.exp(s - m)
                dens.append(jnp.sum(p, axis=-1, keepdims=True))
                maxes.append(m)
                p_scr[2 * u + hh, :, :nk] = p.astype(BF16)
            acc_m[g, q_rows(r, b), :] = jnp.where(low, maxes[0], maxes[1])
            acc_d[g, q_rows(r, b), :] = jnp.where(low, dens[0], dens[1])
        for u, (r, b) in enumerate(units):
            nk = BAND if b == 0 else 2 * BAND
            v = v_refs[g][k_rows(r, b), :].astype(BF16)
            pv = [jnp.dot(p_scr[2 * u + hh, :, :nk], v, preferred_element_type=F32) for hh in range(2)]
            acc_o[g, q_rows(r, b), :] = jnp.where(low, pv[0], pv[1])

    chunk = 256
    for c in range(SEQ // chunk):
        rws = pl.ds(c * chunk, chunk)
        m = [acc_m[g, rws, :] for g in range(N_DIL)]
        top = jnp.maximum(jnp.maximum(m[0], m[1]), m[2])
        num = jnp.zeros((chunk, LANES), F32)
        den = jnp.zeros((chunk, LANES), F32)
        for g in range(N_DIL):
            w = jnp.exp(m[g] - top)
            num = num + w * acc_o[g, rws, :]
            den = den + w * acc_d[g, rws, :]
        o_ref[rws, :] = (num / den).astype(o_ref.dtype)


def attn_prompt(proj, slopes):
    in_specs = [pl.BlockSpec(memory_space=pltpu.SMEM)]
    for g in range(N_DIL):
        for part in range(3):
            col0 = (g * 3 + part) * (D_MODEL // LANES)
            in_specs.append(pl.BlockSpec((SEQ, LANES),
                                         functools.partial(lambda n, hp, c: (n, c + hp), c=col0)))
    return pl.pallas_call(
        _attn_prompt_kernel,
        grid=(BATCH, N_HEADS // 2),
        in_specs=in_specs,
        out_specs=pl.BlockSpec((SEQ, LANES), lambda n, hp: (n, hp)),
        out_shape=jax.ShapeDtypeStruct((T_PROMPT, D_MODEL), BF16),
        scratch_shapes=[pltpu.VMEM((N_DIL, SEQ, LANES), F32)] * 3
        + [pltpu.VMEM((2 * UNITS_PER_GROUP, BAND, 2 * BAND), F32),
           pltpu.VMEM((2 * UNITS_PER_GROUP, BAND, 2 * BAND), BF16)],
        compiler_params=_cparams(2),
        name="attn_prompt",
    )(slopes, *([proj] * 9))


N_A_LAYERS = 2


def _kv_transpose_kernel(a_ref, b_ref, o_ref):
    layer = pl.program_id(0)

    @pl.when(layer == 0)
    def _():
        o_ref[...] = a_ref[...].T

    @pl.when(layer == 1)
    def _():
        o_ref[...] = b_ref[...].T


def kv_transpose(proj_a, proj_b, g, win, name):
    rc = min(win, TILE)
    chunks = win // rc
    blocks_per_seq = SEQ // rc
    first = (SEQ - win) // rc

    def block(n, p, c):
        return (n * blocks_per_seq + first + c, g * 3 + 1 + p)

    a_last = block(BATCH - 1, 1, chunks - 1)
    b_first = block(0, 0, 0)

    def a_map(l, n, p, c):
        r, col = block(n, p, c)
        return (jnp.where(l == 0, r, a_last[0]), jnp.where(l == 0, col, a_last[1]))

    def b_map(l, n, p, c):
        r, col = block(n, p, c)
        return (jnp.where(l == 1, r, b_first[0]), jnp.where(l == 1, col, b_first[1]))

    return pl.pallas_call(
        _kv_transpose_kernel,
        grid=(N_A_LAYERS, BATCH, 2, chunks),
        in_specs=[pl.BlockSpec((rc, D_MODEL), a_map),
                  pl.BlockSpec((rc, D_MODEL), b_map)],
        out_specs=pl.BlockSpec((None, None, None, D_MODEL, rc), lambda l, n, p, c: (l, n, p, 0, c)),
        out_shape=jax.ShapeDtypeStruct((N_A_LAYERS, BATCH, 2, D_MODEL, win), F32),
        compiler_params=_cparams(4),
        name=name,
    )(proj_a, proj_b)


SAMPLE_HB = 8
SAMPLE_CHUNK = 512


def _attn_sample_kernel(slopes_ref, qkv_ref, c0_ref, c1_ref, c2_ref, o_ref):
    hb = pl.program_id(0)
    n = pl.program_id(1)
    caches = (c0_ref, c1_ref, c2_ref)

    @pl.when(n == 0)
    def _():
        o_ref[...] = jnp.zeros_like(o_ref)

    is_n = lax.broadcasted_iota(jnp.int32, (HEAD_DIM, DEC_BATCH), 1) == n

    def column(g, part, h):
        return jnp.sum(jnp.where(is_n, qkv_ref[g, part, h], 0.0), axis=1, keepdims=True)

    heads = range(SAMPLE_HB)
    row = lax.broadcasted_iota(jnp.int32, (SAMPLE_HB, 1), 0)
    neg_slope = jnp.zeros((SAMPLE_HB, 1), F32)
    for h in heads:
        neg_slope = jnp.where(row == h, -slopes_ref[hb * SAMPLE_HB + h], neg_slope)

    ms, nums, dens = [], [], []
    for g, (win, dil) in enumerate(DIL_PAIRS):
        chunk = min(win, SAMPLE_CHUNK)
        n_chunks = win // chunk
        q = [column(g, 0, h) for h in heads]
        s_new = jnp.concatenate([jnp.sum(q[h] * column(g, 1, h), axis=0, keepdims=True) for h in heads],
                                axis=0) * SCALE
        scores = []
        m = s_new
        for c in range(n_chunks):
            lanes = slice(c * chunk, (c + 1) * chunk)
            s = jnp.concatenate([jnp.sum(caches[g][0, h, :, lanes] * q[h], axis=0, keepdims=True)
                                 for h in heads], axis=0) * SCALE
            pos = c * chunk + lax.broadcasted_iota(jnp.int32, (1, chunk), 1)
            s = s + (win - pos).astype(F32) * neg_slope
            if dil > 1:
                s = jnp.where((pos & (dil - 1)) == 0, s, NEG_INF)
            scores.append(s)
            m = jnp.maximum(m, jnp.max(s, axis=1, keepdims=True))
        p_new = jnp.exp(s_new - m)
        den = p_new
        probs = []
        for s in scores:
            p = jnp.exp(s - m)
            den = den + jnp.sum(p, axis=1, keepdims=True)
            probs.append(p)
        group_nums = []
        for h in heads:
            acc = jnp.zeros((HEAD_DIM, chunk), F32)
            for c in range(n_chunks):
                acc = acc + caches[g][1, h, :, c * chunk:(c + 1) * chunk] * probs[c][h:h + 1, :]
            group_nums.append(jnp.sum(acc, axis=1, keepdims=True) + p_new[h:h + 1, :] * column(g, 2, h))
        nums.append(group_nums)
        dens.append(den)
        ms.append(m)

    top = jnp.maximum(jnp.maximum(ms[0], ms[1]), ms[2])
    weights = [jnp.exp(ms[g] - top) for g in range(N_DIL)]
    den = weights[0] * dens[0] + weights[1] * dens[1] + weights[2] * dens[2]
    for h in heads:
        num = jnp.zeros((HEAD_DIM, 1), F32)
        for g in range(N_DIL):
            num = num + weights[g][h:h + 1, :] * nums[g][h]
        o_ref[h] = jnp.where(is_n, num / den[h:h + 1, :], o_ref[h])


def attn_sample(qkv_t, caches, layer, slopes):
    hb = SAMPLE_HB
    in_specs = [pl.BlockSpec(memory_space=pltpu.SMEM),
                pl.BlockSpec((N_DIL, 3, hb, HEAD_DIM, DEC_BATCH), lambda b, n: (0, 0, b, 0, 0))]
    views = []
    for cache, (win, _) in zip(caches, DIL_PAIRS):
        views.append(jnp.transpose(cache, (0, 1, 3, 4, 5, 2)))
        in_specs.append(pl.BlockSpec((None, None, 2, hb, HEAD_DIM, win),
                                     lambda b, n: (layer, n, 0, b, 0, 0)))
    return pl.pallas_call(
        _attn_sample_kernel,
        grid=(N_HEADS // hb, DEC_BATCH),
        in_specs=in_specs,
        out_specs=pl.BlockSpec((hb, HEAD_DIM, DEC_BATCH), lambda b, n: (b, 0, 0)),
        out_shape=jax.ShapeDtypeStruct((N_HEADS, HEAD_DIM, DEC_BATCH), F32),
        compiler_params=_cparams(2),
        name="attn_sample",
    )(slopes, qkv_t, *views)


HIST = 16


def _pool_prompt_kernel(u_ref, wg_ref, sc_ref, z_ref, ext_ref, wgb_ref):
    i = pl.program_id(0)
    tiles_per_seq = SEQ // TILE

    @pl.when(i == 0)
    def _():
        wgb_ref[...] = wg_ref[...].astype(BF16)

    @pl.when(i % tiles_per_seq == 0)
    def _():
        ext_ref[0:HIST, :] = jnp.zeros((HIST, D_MODEL), F32)

    ext_ref[HIST:HIST + TILE, :] = u_ref[...]
    pos = (i % tiles_per_seq) * TILE + lax.broadcasted_iota(jnp.int32, (TILE, 1), 0)
    for g, win in enumerate(POOL_WINDOWS):
        cols = slice(g * POOL_CH, (g + 1) * POOL_CH)
        acc = ext_ref[HIST:HIST + TILE, cols]
        for k in range(1, win):
            acc = acc + ext_ref[HIST - k:HIST - k + TILE, cols]
        cnt = jnp.minimum(win, pos + 1).astype(F32)
        pooled = acc / cnt - u_ref[:, cols]
        z = jnp.dot(pooled.astype(BF16), wgb_ref[g], preferred_element_type=F32)
        z_ref[:, cols] = (z * sc_ref[:, cols]).astype(z_ref.dtype)
    ext_ref[0:HIST, :] = ext_ref[TILE:TILE + HIST, :]


def pool_prompt(u, w_group, scale):
    return pl.pallas_call(
        _pool_prompt_kernel,
        grid=(T_PROMPT // TILE,),
        in_specs=[pl.BlockSpec((TILE, D_MODEL), lambda i: (i, 0)),
                  pl.BlockSpec((None, len(POOL_WINDOWS), POOL_CH, POOL_CH), lambda i: (0, 0, 0, 0)),
                  pl.BlockSpec((1, D_MODEL), lambda i: (0, 0))],
        out_specs=pl.BlockSpec((TILE, D_MODEL), lambda i: (i, 0)),
        out_shape=jax.ShapeDtypeStruct((T_PROMPT, D_MODEL), BF16),
        scratch_shapes=[pltpu.VMEM((HIST + TILE, D_MODEL), F32),
                        pltpu.VMEM((len(POOL_WINDOWS), POOL_CH, POOL_CH), BF16)],
        compiler_params=_cparams(1),
        name="pool_prompt",
    )(u, w_group, scale)


def _pool_sample_kernel(u_ref, st_ref, wg_ref, sc_ref, z_ref):
    for g, win in enumerate(POOL_WINDOWS):
        cols = slice(g * POOL_CH, (g + 1) * POOL_CH)
        u = u_ref[:, cols]
        acc = u
        for k in range(1, win):
            acc = acc + st_ref[POOL_STATE - k, :, cols]
        cnt = float(min(win, PAST_LEN + 1))
        pooled = acc / cnt - u
        z = jnp.dot(pooled.astype(BF16), wg_ref[g].astype(BF16), preferred_element_type=F32)
        z_ref[:, cols] = (z * sc_ref[:, cols]).astype(z_ref.dtype)


def pool_sample(u, state_t, w_group, scale):
    return pl.pallas_call(
        _pool_sample_kernel,
        grid=(1,),
        in_specs=[pl.BlockSpec((DEC_BATCH, D_MODEL), lambda i: (0, 0)),
                  pl.BlockSpec((POOL_STATE, DEC_BATCH, D_MODEL), lambda i: (0, 0, 0)),
                  pl.BlockSpec((None, len(POOL_WINDOWS), POOL_CH, POOL_CH), lambda i: (0, 0, 0, 0)),
                  pl.BlockSpec((1, D_MODEL), lambda i: (0, 0))],
        out_specs=pl.BlockSpec((DEC_BATCH, D_MODEL), lambda i: (0, 0)),
        out_shape=jax.ShapeDtypeStruct((DEC_BATCH, D_MODEL), BF16),
        compiler_params=_cparams(1),
        name="pool_sample",
    )(u, state_t, w_group, scale)


CONV_HIST = 8


def _conv_prompt_kernel(p_ref, cw_ref, y_ref, ch_ref, ext_ref):
    i = pl.program_id(0)

    @pl.when(i % (SEQ // TILE) == 0)
    def _():
        ext_ref[0:CONV_HIST, :] = jnp.zeros((CONV_HIST, D_MODEL), F32)

    ch = p_ref[:, D_MODEL:2 * D_MODEL] * p_ref[:, 2 * D_MODEL:]
    ext_ref[CONV_HIST:CONV_HIST + TILE, :] = ch
    conv = cw_ref[0:1, :] * ext_ref[CONV_HIST - 2:CONV_HIST - 2 + TILE, :]
    conv = conv + cw_ref[1:2, :] * ext_ref[CONV_HIST - 1:CONV_HIST - 1 + TILE, :]
    conv = conv + cw_ref[2:3, :] * ch
    y_ref[...] = (p_ref[:, :D_MODEL] * conv).astype(y_ref.dtype)
    ch_ref[...] = ch
    ext_ref[0:CONV_HIST, :] = ext_ref[TILE:TILE + CONV_HIST, :]


def conv_prompt(p, conv_w):
    return pl.pallas_call(
        _conv_prompt_kernel,
        grid=(T_PROMPT // TILE,),
        in_specs=[pl.BlockSpec((TILE, 3 * D_MODEL), lambda i: (i, 0)),
                  pl.BlockSpec((None, 3, D_MODEL), lambda i: (0, 0, 0))],
        out_specs=[pl.BlockSpec((TILE, D_MODEL), lambda i: (i, 0)),
                   pl.BlockSpec((TILE, D_MODEL), lambda i: (i, 0))],
        out_shape=[jax.ShapeDtypeStruct((T_PROMPT, D_MODEL), BF16),
                   jax.ShapeDtypeStruct((T_PROMPT, D_MODEL), F32)],
        scratch_shapes=[pltpu.VMEM((CONV_HIST + TILE, D_MODEL), F32)],
        compiler_params=_cparams(1),
        name="conv_prompt",
    )(p, conv_w)


def _conv_sample_kernel(p_ref, st_ref, cw_ref, y_ref, ch_ref):
    ch = p_ref[:, D_MODEL:2 * D_MODEL] * p_ref[:, 2 * D_MODEL:]
    conv = cw_ref[0:1, :] * st_ref[0]
    conv = conv + cw_ref[1:2, :] * st_ref[1]
    conv = conv + cw_ref[2:3, :] * ch
    y_ref[...] = (p_ref[:, :D_MODEL] * conv).astype(y_ref.dtype)
    ch_ref[...] = ch


def conv_sample(p, state_t, conv_w):
    return pl.pallas_call(
        _conv_sample_kernel,
        grid=(1,),
        in_specs=[pl.BlockSpec((DEC_BATCH, 3 * D_MODEL), lambda i: (0, 0)),
                  pl.BlockSpec((2, DEC_BATCH, D_MODEL), lambda i: (0, 0, 0)),
                  pl.BlockSpec((None, 3, D_MODEL), lambda i: (0, 0, 0))],
        out_specs=[pl.BlockSpec((DEC_BATCH, D_MODEL), lambda i: (0, 0)),
                   pl.BlockSpec((DEC_BATCH, D_MODEL), lambda i: (0, 0))],
        out_shape=[jax.ShapeDtypeStruct((DEC_BATCH, D_MODEL), BF16),
                   jax.ShapeDtypeStruct((DEC_BATCH, D_MODEL), F32)],
        compiler_params=_cparams(1),
        name="conv_sample",
    )(p, state_t, conv_w)


def _tail_rows(sample_rows):
    return jnp.pad(sample_rows.astype(BF16), ((0, TAIL - DEC_BATCH), (0, 0)))


def kernel(x_prompt, x_sample, cache_kv_w128, cache_kv_w512, cache_kv_w2048, state_pool, state_conv, a_w_in, a_w_out, b_w_in, b_w_group, b_scale, b_w_out, c_w_in, c_conv, c_w_out, ln1_g, ln1_b, ln2_g, ln2_b, moe_w_group, moe_b_group, moe_w_expert, moe_b_expert, moe_w_up, moe_w_down):
    caches = (cache_kv_w128, cache_kv_w512, cache_kv_w2048)
    heads = jnp.arange(1, N_HEADS + 1, dtype=F32)
    slopes = jnp.exp2(-8.0 * heads / N_HEADS)

    x = jnp.concatenate([x_prompt.reshape(T_PROMPT, D_MODEL),
                         x_sample.reshape(DEC_BATCH, D_MODEL),
                         jnp.zeros((T_PAD - T_REAL, D_MODEL), F32)], axis=0)
    xb = x.astype(BF16)

    projs = []
    kv_s = [[] for _ in DIL_PAIRS]
    pool_p = pool_s = conv_p = conv_s = None

    for i in range(DEPTH):
        j, kind = i // N_MIXERS, i % N_MIXERS
        if kind == 0:
            proj, proj_t = project(xb, a_w_in, j, 1536, name=f"a_in_{i}")
            o = attn_prompt(proj, slopes)
            qkv_s = proj_t[:DEC_BATCH].reshape(DEC_BATCH, N_DIL, 3, N_HEADS, HEAD_DIM)
            o_t = attn_sample(jnp.transpose(qkv_s, (1, 2, 3, 4, 0)), caches, j, slopes)
            mixed, tail = o, _tail_rows(jnp.transpose(o_t, (2, 0, 1)).reshape(DEC_BATCH, D_MODEL))
            w_out, layer_w = a_w_out, j
            projs.append(proj)
            for g in range(N_DIL):
                kv_s[g].append(qkv_s[:, None, g, 1:3])
        elif kind == 1:
            u, u_t = project(xb, b_w_in, j, 1024, name=f"b_in_{i}")
            z = pool_prompt(u, b_w_group[j:j + 1], b_scale[j:j + 1])
            state_t = jnp.swapaxes(state_pool[j], 0, 1)
            z_s = pool_sample(u_t, state_t, b_w_group[j:j + 1], b_scale[j:j + 1])
            mixed, tail = z, _tail_rows(z_s)
            w_out, layer_w = b_w_out, j
            u_p = u.reshape(BATCH, SEQ, D_MODEL)
            u_s = u_t[:DEC_BATCH]
            pool_p = u_p[:, SEQ - POOL_STATE:][None]
            pool_s = jnp.concatenate([state_pool[j][:, 1:], u_s[:, None]], axis=1)[None]
        else:
            p3, p3_t = project(xb, c_w_in, j, 1536, name=f"c_in_{i}")
            y, ch = conv_prompt(p3, c_conv[j:j + 1])
            state_t = jnp.swapaxes(state_conv[j], 0, 1)
            y_s, ch_s = conv_sample(p3_t, state_t, c_conv[j:j + 1])
            mixed, tail = y, _tail_rows(y_s)
            w_out, layer_w = c_w_out, j
            conv_p = ch.reshape(BATCH, SEQ, D_MODEL)[:, SEQ - 2:][None]
            conv_s = jnp.concatenate([state_conv[j][:, 1:], ch_s[:, None]], axis=1)[None]

        wr = jnp.concatenate(
            [moe_w_group[i],
             jnp.transpose(moe_w_expert[i], (1, 0, 2)).reshape(D_MODEL, N_EXPERTS),
             jnp.zeros((D_MODEL, LANES - N_GROUPS - N_EXPERTS), F32)], axis=1)
        br = jnp.concatenate([moe_b_group[i], moe_b_expert[i].reshape(N_EXPERTS),
                              jnp.zeros((LANES - N_GROUPS - N_EXPERTS,), F32)])[None]
        h, hb, rw, ri = mix_out(mixed, tail, w_out, layer_w, x, ln1_g[i:i + 1], ln1_b[i:i + 1],
                                wr, br, name=f"mix_out_{i}")
        x, xb = sparse_moe(h, rw, ri, moe_w_up, moe_w_down, i, ln2_g[i:i + 1], ln2_b[i:i + 1], i)

    y_prompt = x[:T_PROMPT].reshape(BATCH, SEQ, D_MODEL)
    y_sample = x[T_PROMPT:T_REAL].reshape(DEC_BATCH, 1, D_MODEL)
    kv_p = []
    for g, (win, _) in enumerate(DIL_PAIRS):
        kv_t = kv_transpose(projs[0], projs[1], g, win, name=f"kv_out_{g}")
        kv_t = kv_t.reshape(N_A_LAYERS, BATCH, 2, N_HEADS, HEAD_DIM, win)
        kv_p.append(jnp.transpose(kv_t, (0, 1, 5, 2, 3, 4)))
    stack = lambda rows: jnp.stack(rows)
    return (y_prompt, y_sample,
            kv_p[0], stack(kv_s[0]), kv_p[1], stack(kv_s[1]),
            kv_p[2], stack(kv_s[2]), pool_p, pool_s, conv_p, conv_s)
```
